```python
import jax, jax.numpy as jnp
from jax import lax
import numpy as np

D_MODEL = 1024
BATCH = 16
SEQ = 4096
DEPTH = 4

N_MIXERS = 2
N_CONV_LAYERS = (DEPTH + 1) // 2
N_ATTN_LAYERS = DEPTH // 2
CONV_W = 3
N_HEADS = 16
HEAD_DIM = D_MODEL // N_HEADS
Q_BLOCK = 128
N_EXPERTS = 32
TOP_K = 4
D_FF = D_MODEL
SWIGLU_LIMIT = 7.0
SWIGLU_ALPHA = 1.702
EXPERT_BLOCK = 512
EPS = 1e-6

kernel_name = "hybrid_shortconv_stickbreaking_moe_adaln"


def rms_norm(x, g):
    xf = x.astype(jnp.float32)
    y = xf * lax.rsqrt(jnp.mean(xf * xf, axis=-1, keepdims=True) + EPS)
    return (y * g.astype(jnp.float32)).astype(x.dtype)


def short_conv_mixer(h, w_in, conv_w, w_out):
    S = h.shape[1]
    proj = h @ w_in
    b_gate, c_gate, u = jnp.split(proj, 3, axis=-1)
    v = c_gate * u
    vp = jnp.pad(v, ((0, 0), (CONV_W - 1, 0), (0, 0)))
    y = conv_w[CONV_W - 1] * v
    for k in range(CONV_W - 1):
        y = y + conv_w[k] * vp[:, k:k + S]
    return (b_gate * y) @ w_out


def stick_breaking_mixer(h, w_qkv, q_g, k_g, w_o):
    B, S, D = h.shape
    qkv = (h @ w_qkv).reshape(B, S, 3, N_HEADS, HEAD_DIM)
    q = rms_norm(qkv[:, :, 0], q_g).transpose(0, 2, 1, 3)
    k = rms_norm(qkv[:, :, 1], k_g).transpose(0, 2, 1, 3)
    v = qkv[:, :, 2].transpose(0, 2, 1, 3)
    n_blocks = S // Q_BLOCK
    q_blocks = q.reshape(B, N_HEADS, n_blocks, Q_BLOCK, HEAD_DIM).transpose(2, 0, 1, 3, 4)
    k32 = k.astype(jnp.float32)
    key_pos = jnp.arange(S)
    scale = HEAD_DIM ** -0.5

    def block(args):
        i, qi = args
        z = jnp.einsum('bhqd,bhkd->bhqk', qi.astype(jnp.float32), k32) * scale
        t = i * Q_BLOCK + jnp.arange(Q_BLOCK)
        mask = key_pos[None, :] < t[:, None]
        log_beta = jax.nn.log_sigmoid(z)
        log_1m = jnp.where(mask, log_beta - z, 0.0)
        suffix = lax.cumsum(log_1m, axis=3, reverse=True) - log_1m
        a = jnp.where(mask, jnp.exp(log_beta + suffix), 0.0)
        return jnp.einsum('bhqk,bhkd->bhqd', a.astype(v.dtype), v)

    o = lax.map(block, (jnp.arange(n_blocks), q_blocks))
    o = o.transpose(1, 0, 3, 2, 4).reshape(B, S, D)
    return o @ w_o


def moe_ffn(h, router_w, router_b, w1, b1, w2, b2):
    B, S, D = h.shape
    xt = h.reshape(-1, D)
    n_tok = xt.shape[0]
    logits = (xt @ router_w).astype(jnp.float32) + router_b.astype(jnp.float32)
    top_logits, top_idx = lax.top_k(logits, TOP_K)
    gates = jax.nn.softmax(top_logits, axis=-1)

    n_assign = n_tok * TOP_K
    e_flat = top_idx.reshape(-1)
    order = jnp.argsort(e_flat)
    e_sorted = e_flat[order]
    tok_sorted = (order // TOP_K).astype(jnp.int32)
    g_sorted = gates.reshape(-1)[order]
    counts = jnp.bincount(e_flat, length=N_EXPERTS)
    padded = (counts + EXPERT_BLOCK - 1) // EXPERT_BLOCK * EXPERT_BLOCK
    start = jnp.cumsum(counts) - counts
    padded_end = jnp.cumsum(padded)
    padded_start = padded_end - padded
    dest = padded_start[e_sorted] + jnp.arange(n_assign) - start[e_sorted]
    n_blocks = -(-n_assign // EXPERT_BLOCK) + N_EXPERTS
    n_pad = n_blocks * EXPERT_BLOCK
    pad_tok = jnp.full((n_pad,), n_tok, jnp.int32).at[dest].set(tok_sorted)
    pad_gate = jnp.zeros((n_pad,), jnp.float32).at[dest].set(g_sorted)
    block_expert = jnp.clip(
        jnp.searchsorted(padded_end, jnp.arange(n_blocks) * EXPERT_BLOCK, side='right'),
        0, N_EXPERTS - 1)
    x_pad = jnp.concatenate([xt, jnp.zeros((1, D), xt.dtype)], axis=0)

    def step(acc, inp):
        e, tok, g = inp
        xb = x_pad[tok]
        gu = xb @ w1[e] + b1[e]
        g_h = jnp.minimum(gu[:, :D_FF], SWIGLU_LIMIT)
        u_h = jnp.clip(gu[:, D_FF:], -SWIGLU_LIMIT, SWIGLU_LIMIT)
        act = (u_h + 1.0) * (g_h * jax.nn.sigmoid(SWIGLU_ALPHA * g_h))
        y = act @ w2[e] + b2[e]
        return acc.at[tok].add(y * g[:, None].astype(y.dtype)), None

    acc, _ = lax.scan(step, jnp.zeros((n_tok + 1, D), h.dtype),
                      (block_expert, pad_tok.reshape(n_blocks, EXPERT_BLOCK),
                       pad_gate.reshape(n_blocks, EXPERT_BLOCK)))
    return acc[:n_tok].reshape(B, S, D)


def setup_inputs(seed: int = 0) -> dict:
    key = jax.random.key(seed)
    ks = jax.random.split(key, 20)
    D, E, F = D_MODEL, N_EXPERTS, D_FF
    nrm = lambda k, shape, s: jax.random.normal(k, shape, jnp.float32) * s
    return {
        "x": nrm(ks[0], (BATCH, SEQ, D), 1.0),
        "c": nrm(ks[1], (BATCH, D), 1.0),
        "norm1_g": 1.0 + nrm(ks[2], (DEPTH, D), 0.02),
        "norm2_g": 1.0 + nrm(ks[3], (DEPTH, D), 0.02),
        "ada_w": nrm(ks[4], (DEPTH, D, 6 * D), 0.5 * D ** -0.5),
        "ada_b": nrm(ks[5], (DEPTH, 6 * D), 0.02),
        "conv_w_in": nrm(ks[6], (N_CONV_LAYERS, D, 3 * D), D ** -0.5),
        "conv_w": nrm(ks[7], (N_CONV_LAYERS, CONV_W, D), CONV_W ** -0.5),
        "conv_w_out": nrm(ks[8], (N_CONV_LAYERS, D, D), D ** -0.5),
        "attn_w_qkv": nrm(ks[9], (N_ATTN_LAYERS, D, 3 * D), D ** -0.5),
        "attn_q_g": 1.0 + nrm(ks[10], (N_ATTN_LAYERS, HEAD_DIM), 0.02),
        "attn_k_g": 1.0 + nrm(ks[11], (N_ATTN_LAYERS, HEAD_DIM), 0.02),
        "attn_w_o": nrm(ks[12], (N_ATTN_LAYERS, D, D), D ** -0.5),
        "router_w": nrm(ks[13], (DEPTH, D, E), D ** -0.5),
        "router_b": nrm(ks[14], (DEPTH, E), 0.01),
        "moe_w1": nrm(ks[15], (DEPTH, E, D, 2 * F), D ** -0.5),
        "moe_b1": nrm(ks[16], (DEPTH, E, 2 * F), 0.02),
        "moe_w2": nrm(ks[17], (DEPTH, E, F, D), F ** -0.5),
        "moe_b2": nrm(ks[18], (DEPTH, E, D), 0.02),
    }


def reference(x, c, norm1_g, norm2_g, ada_w, ada_b, conv_w_in, conv_w, conv_w_out,
              attn_w_qkv, attn_q_g, attn_k_g, attn_w_o, router_w, router_b,
              moe_w1, moe_b1, moe_w2, moe_b2):
    c_act = jax.nn.silu(c)
    for i in range(DEPTH):
        mod = c_act @ ada_w[i] + ada_b[i]
        sh1, sc1, gt1, sh2, sc2, gt2 = [m[:, None, :] for m in jnp.split(mod, 6, axis=-1)]
        h = rms_norm(x, norm1_g[i]) * (1.0 + sc1) + sh1
        j = i // N_MIXERS
        if i % N_MIXERS == 0:
            y = short_conv_mixer(h, conv_w_in[j], conv_w[j], conv_w_out[j])
        else:
            y = stick_breaking_mixer(h, attn_w_qkv[j], attn_q_g[j], attn_k_g[j], attn_w_o[j])
        x = x + gt1 * y
        h = rms_norm(x, norm2_g[i]) * (1.0 + sc2) + sh2
        x = x + gt2 * moe_ffn(h, router_w[i], router_b[i], moe_w1[i], moe_b1[i],
                               moe_w2[i], moe_b2[i])
    return x
```

```python
import functools

import jax
import jax.numpy as jnp
from jax import lax
from jax.experimental import pallas as pl
from jax.experimental.pallas import tpu as pltpu

HEAD_DIM = 64
TOP_K = 4
CONV_W = 3
EPS = 1e-6
SWIGLU_LIMIT = 7.0
SWIGLU_ALPHA = 1.702
LANES = 128
HEADS_PER_TILE = LANES // HEAD_DIM
NEG_CUTOFF = -90.0
VMEM_LIMIT = 56 * 1024 * 1024

F32 = jnp.float32
BF16 = jnp.bfloat16
_NT = (((1,), (1,)), ((), ()))


def _dot(a, b):
    return jnp.dot(a, b, preferred_element_type=F32)


def _split(a):
    hi = a.astype(BF16)
    lo = (a - hi.astype(F32)).astype(BF16)
    return hi, lo


def _modulate(x, g, sc, sh):
    ms = jnp.mean(x * x, axis=-1, keepdims=True)
    return (x * lax.rsqrt(ms + EPS)) * g * (1.0 + sc) + sh


def _params(*sem):
    return pltpu.CompilerParams(dimension_semantics=sem, vmem_limit_bytes=VMEM_LIMIT)


def _ada_kernel(c_ref, w_ref, b_ref, o_ref):
    c = c_ref[...]
    ca = c * jax.nn.sigmoid(c)
    o_ref[0] = jnp.dot(ca, w_ref[0], preferred_element_type=F32,
                       precision=lax.Precision.HIGHEST) + b_ref[0]


def _ada(c, ada_w, ada_b):
    depth, d, d6 = ada_w.shape
    b = c.shape[0]
    tn = min(d6, 1536)
    out = pl.pallas_call(
        _ada_kernel,
        grid=(depth, d6 // tn),
        in_specs=[pl.BlockSpec((b, d), lambda i, j: (0, 0)),
                  pl.BlockSpec((1, d, tn), lambda i, j: (i, 0, j)),
                  pl.BlockSpec((1, 1, tn), lambda i, j: (i, 0, j))],
        out_specs=pl.BlockSpec((1, b, tn), lambda i, j: (i, 0, j)),
        out_shape=jax.ShapeDtypeStruct((depth, b, d6), F32),
        compiler_params=_params("parallel", "parallel"),
        name="ada",
    )(c, ada_w, ada_b.reshape(depth, 1, d6))
    return out.reshape(depth, b, 6, d)


def _conv_kernel(x_ref, mod_ref, g_ref, win_ref, cw_ref, wout_ref, o_ref, carry_ref, *, ts, d):
    s = pl.program_id(1)

    @pl.when(s == 0)
    def _():
        carry_ref[...] = jnp.zeros_like(carry_ref)

    x = x_ref[0]
    h = _modulate(x, g_ref[...], mod_ref[0, 1:2, :], mod_ref[0, 0:1, :]).astype(BF16)
    proj = _dot(h, win_ref[...])
    v = proj[:, d:2 * d] * proj[:, 2 * d:]
    prev = carry_ref[...]
    row = lax.broadcasted_iota(jnp.int32, (ts, 1), 0)
    v1 = jnp.where(row == 0, prev[7:8, :], pltpu.roll(v, 1, axis=0))
    v2 = jnp.where(row == 0, prev[6:7, :],
                   jnp.where(row == 1, prev[7:8, :], pltpu.roll(v, 2, axis=0)))
    y = cw_ref[2:3, :] * v + cw_ref[1:2, :] * v1 + cw_ref[0:1, :] * v2
    carry_ref[...] = v[ts - 8:, :]
    out = _dot((proj[:, :d] * y).astype(BF16), wout_ref[...])
    o_ref[0] = x + mod_ref[0, 2:3, :] * out


def _conv_mixer(x, mod, g1, w_in, conv_w, w_out, ts):
    b, s, d = x.shape
    return pl.pallas_call(
        functools.partial(_conv_kernel, ts=ts, d=d),
        grid=(b, s // ts),
        in_specs=[pl.BlockSpec((1, ts, d), lambda i, j: (i, j, 0)),
                  pl.BlockSpec((1, 6, d), lambda i, j: (i, 0, 0)),
                  pl.BlockSpec((1, d), lambda i, j: (0, 0)),
                  pl.BlockSpec((d, 3 * d), lambda i, j: (0, 0)),
                  pl.BlockSpec((CONV_W, d), lambda i, j: (0, 0)),
                  pl.BlockSpec((d, d), lambda i, j: (0, 0))],
        out_specs=pl.BlockSpec((1, ts, d), lambda i, j: (i, j, 0)),
        out_shape=jax.ShapeDtypeStruct((b, s, d), F32),
        scratch_shapes=[pltpu.VMEM((8, d), F32)],
        compiler_params=_params("parallel", "arbitrary"),
        name="conv_mixer",
    )(x, mod, g1, w_in, conv_w, w_out)


def _head_rms(t, gsum_ref, gexp_ref):
    hi, lo = _split(t * t)
    ss = _dot(hi, gsum_ref[...]) + _dot(lo, gsum_ref[...])
    inv = lax.rsqrt(ss * (1.0 / HEAD_DIM) + EPS)
    ihi, ilo = _split(inv)
    return _dot(ihi, gexp_ref[...]) + _dot(ilo, gexp_ref[...])


def _qkv_kernel(x_ref, mod_ref, g_ref, w_ref, qg_ref, kg_ref, gsum_ref, gexp_ref,
                q_ref, k_ref, v_ref, *, d):
    x = x_ref[0]
    h = _modulate(x, g_ref[...], mod_ref[0, 1:2, :], mod_ref[0, 0:1, :]).astype(BF16)
    qkv = _dot(h, w_ref[...])
    q = qkv[:, :d]
    k = qkv[:, d:2 * d]
    q_ref[0] = (q * _head_rms(q, gsum_ref, gexp_ref) * (qg_ref[...] * HEAD_DIM ** -0.5)).astype(BF16)
    k_ref[0] = (k * _head_rms(k, gsum_ref, gexp_ref) * kg_ref[...]).astype(BF16)
    v_ref[0] = qkv[:, 2 * d:].astype(BF16)


def _qkv(x, mod, g1, w_qkv, q_g, k_g, ts):
    b, s, d = x.shape
    n_heads = d // HEAD_DIM
    head_of = jnp.arange(d) // HEAD_DIM
    gsum = (head_of[:, None] == jnp.arange(LANES)[None, :]).astype(BF16)
    gexp = gsum.T
    qg = jnp.tile(q_g, n_heads)[None, :]
    kg = jnp.tile(k_g, n_heads)[None, :]
    blk = pl.BlockSpec((1, ts, d), lambda i, j: (i, j, 0))
    full = lambda shape: pl.BlockSpec(shape, lambda i, j: (0,) * len(shape))
    return pl.pallas_call(
        functools.partial(_qkv_kernel, d=d),
        grid=(b, s // ts),
        in_specs=[blk, pl.BlockSpec((1, 6, d), lambda i, j: (i, 0, 0)), full((1, d)),
                  full((d, 3 * d)), full((1, d)), full((1, d)), full((d, LANES)), full((LANES, d))],
        out_specs=[blk, blk, blk],
        out_shape=[jax.ShapeDtypeStruct((b, s, d), BF16)] * 3,
        compiler_params=_params("parallel", "parallel"),
        name="qkv",
    )(x, mod, g1, w_qkv, qg, kg, gsum, gexp)


def _attn_kernel(q_ref, k_ref, v_ref, t_ref, o_ref, *, tq):
    qi = pl.program_id(2)
    q = q_ref[0]
    lane = lax.broadcasted_iota(jnp.int32, (1, LANES), 1)
    row = lax.broadcasted_iota(jnp.int32, (tq, tq), 0)
    col = lax.broadcasted_iota(jnp.int32, (tq, tq), 1)
    causal = col < row
    n_chunk = tq // LANES
    result = None
    for j in range(HEADS_PER_TILE):
        in_head = (lane // HEAD_DIM) == j
        qj = jnp.where(in_head, q, jnp.zeros_like(q))

        def block(kb, carry, acc, masked, qj=qj):
            start = pl.multiple_of(kb * tq, tq)
            kblk = k_ref[0, pl.ds(start, tq), :]
            vblk = v_ref[0, pl.ds(start, tq), :]
            z = lax.dot_general(qj, kblk, _NT, preferred_element_type=F32)
            l1m = -(jnp.maximum(z, 0.0) + jnp.log(1.0 + jnp.exp(-jnp.abs(z))))
            if masked:
                l1m = jnp.where(causal, l1m, 0.0)
            suf = _dot(l1m.astype(BF16), t_ref[...])
            expo = (z + l1m) + suf[:, :tq] + jnp.concatenate([carry] * n_chunk, axis=1)
            a = jnp.exp(expo)
            if masked:
                a = jnp.where(causal, a, 0.0)
            acc = acc + _dot(a.astype(BF16), vblk)
            return carry + suf[:, tq:], acc

        zero = jnp.zeros((tq, LANES), F32)
        carry, acc = block(qi, zero, zero, True)

        def cond(st):
            kb, carry, _ = st
            return jnp.logical_and(kb >= 0, jnp.max(carry) > NEG_CUTOFF)

        def body(st):
            kb, carry, acc = st
            carry, acc = block(kb, carry, acc, False)
            return kb - 1, carry, acc

        _, _, acc = lax.while_loop(cond, body, (qi - 1, carry, acc))
        result = acc if result is None else jnp.where(in_head, acc, result)
    o_ref[0] = result.astype(o_ref.dtype)


def _attention(q, k, v, tq):
    b, s, d = q.shape
    jj = jnp.arange(tq)
    tri = (jj[:, None] > jj[None, :]).astype(BF16)
    text = jnp.concatenate([tri, jnp.ones((tq, LANES), BF16)], axis=1)
    return pl.pallas_call(
        functools.partial(_attn_kernel, tq=tq),
        grid=(b, d // LANES, s // tq),
        in_specs=[pl.BlockSpec((1, tq, LANES), lambda i, h, j: (i, j, h)),
                  pl.BlockSpec((1, s, LANES), lambda i, h, j: (i, 0, h)),
                  pl.BlockSpec((1, s, LANES), lambda i, h, j: (i, 0, h)),
                  pl.BlockSpec((tq, tq + LANES), lambda i, h, j: (0, 0))],
        out_specs=pl.BlockSpec((1, tq, LANES), lambda i, h, j: (i, j, h)),
        out_shape=jax.ShapeDtypeStruct((b, s, d), BF16),
        compiler_params=_params("parallel", "parallel", "parallel"),
        name="stickbreak_attn",
    )(q, k, v, text)


def _oproj_kernel(x_ref, o_ref_in, mod_ref, w_ref, out_ref):
    out_ref[0] = x_ref[0] + mod_ref[0, 2:3, :] * _dot(o_ref_in[0], w_ref[...])


def _oproj(x, o, mod, w_o, ts):
    b, s, d = x.shape
    blk = pl.BlockSpec((1, ts, d), lambda i, j: (i, j, 0))
    return pl.pallas_call(
        _oproj_kernel,
        grid=(b, s // ts),
        in_specs=[blk, blk, pl.BlockSpec((1, 6, d), lambda i, j: (i, 0, 0)),
                  pl.BlockSpec((d, d), lambda i, j: (0, 0))],
        out_specs=blk,
        out_shape=jax.ShapeDtypeStruct((b, s, d), F32),
        compiler_params=_params("parallel", "parallel"),
        name="attn_oproj",
    )(x, o, mod, w_o)


def _router_kernel(x_ref, mod_ref, g_ref, rwh_ref, rwl_ref, rb_ref, tri_ref,
                   h_ref, idx_ref, gate_ref, rank_ref, cnt_ref, base_ref, *, n_exp, tr):
    @pl.when(pl.program_id(0) == 0)
    def _():
        base_ref[...] = jnp.zeros_like(base_ref)

    h = _modulate(x_ref[...], g_ref[...], mod_ref[0, 4:5, :], mod_ref[0, 3:4, :])
    h_ref[...] = h
    hh, hl = _split(h)
    rwh = rwh_ref[...]
    logits = (lax.dot_general(rwh, hh, _NT, preferred_element_type=F32)
              + lax.dot_general(rwh, hl, _NT, preferred_element_type=F32)
              + lax.dot_general(rwl_ref[...], hh, _NT, preferred_element_type=F32)) + rb_ref[...]
    eio = lax.broadcasted_iota(jnp.int32, (n_exp, tr), 0)
    work = logits
    tops, onehots = [], []
    for k in range(TOP_K):
        m = jnp.max(work, axis=0, keepdims=True)
        am = jnp.min(jnp.where(work == m, eio, n_exp), axis=0, keepdims=True)
        oh = eio == am
        work = jnp.where(oh, -jnp.inf, work)
        tops.append(m)
        onehots.append(oh)
        idx_ref[k:k + 1, :] = am
    es = [jnp.exp(m - tops[0]) for m in tops]
    den = (es[0] + es[1]) + (es[2] + es[3])
    for k in range(TOP_K):
        gate_ref[k:k + 1, :] = es[k] / den
    sel = jnp.zeros((n_exp, tr), F32)
    for oh in onehots:
        sel = sel + oh.astype(F32)
    cum = _dot(sel.astype(BF16), tri_ref[...])
    base = base_ref[...]
    pos = cum[:, :tr] + jnp.concatenate([base] * (tr // LANES), axis=1)
    for k in range(TOP_K):
        rank_ref[k:k + 1, :] = jnp.sum(jnp.where(onehots[k], pos, 0.0), axis=0,
                                       keepdims=True).astype(jnp.int32)
    base = base + cum[:, tr:]
    base_ref[...] = base
    cnt_ref[...] = base.astype(jnp.int32)


def _router(x2, mod, g2, router_w, router_b, seq, tr):
    n, d = x2.shape
    n_exp = router_w.shape[1]
    rwh, rwl = _split(router_w.T)
    jj = jnp.arange(tr)
    tri = (jj[:, None] < jj[None, :]).astype(BF16)
    text = jnp.concatenate([tri, jnp.ones((tr, LANES), BF16)], axis=1)
    per_b = seq // tr
    full = lambda shape: pl.BlockSpec(shape, lambda i: (0,) * len(shape))
    row4 = pl.BlockSpec((TOP_K, tr), lambda i: (0, i))
    return pl.pallas_call(
        functools.partial(_router_kernel, n_exp=n_exp, tr=tr),
        grid=(n // tr,),
        in_specs=[pl.BlockSpec((tr, d), lambda i: (i, 0)),
                  pl.BlockSpec((1, 6, d), lambda i: (i // per_b, 0, 0)),
                  full((1, d)), full((n_exp, d)), full((n_exp, d)), full((n_exp, 1)),
                  full((tr, tr + LANES))],
        out_specs=[pl.BlockSpec((tr, d), lambda i: (i, 0)), row4, row4, row4,
                   full((n_exp, LANES))],
        out_shape=[jax.ShapeDtypeStruct((n, d), F32),
                   jax.ShapeDtypeStruct((TOP_K, n), jnp.int32),
                   jax.ShapeDtypeStruct((TOP_K, n), F32),
                   jax.ShapeDtypeStruct((TOP_K, n), jnp.int32),
                   jax.ShapeDtypeStruct((n_exp, LANES), jnp.int32)],
        scratch_shapes=[pltpu.VMEM((n_exp, LANES), F32)],
        compiler_params=_params("arbitrary"),
        name="moe_router",
    )(x2, mod, g2, rwh, rwl, router_b[:, None], text)


def _dispatch_kernel(dest_ref, h_ref, xs_ref, sem, *, td):
    def issue(t, c):
        for k in range(TOP_K):
            slot = dest_ref[0, 0, t * TOP_K + k]
            pltpu.make_async_copy(h_ref.at[pl.ds(t, 1)], xs_ref.at[pl.ds(slot, 1)], sem).start()
        return c

    lax.fori_loop(0, td, issue, 0)
    for _ in range(TOP_K):
        pltpu.make_async_copy(h_ref, xs_ref.at[pl.ds(0, td)], sem).wait()


def _dispatch(h, dest_tok, n_pad, td):
    n, d = h.shape
    return pl.pallas_call(
        functools.partial(_dispatch_kernel, td=td),
        grid=(n // td,),
        in_specs=[pl.BlockSpec((1, 1, td * TOP_K), lambda i: (i, 0, 0), memory_space=pltpu.SMEM),
                  pl.BlockSpec((td, d), lambda i: (i, 0))],
        out_specs=pl.BlockSpec(memory_space=pl.ANY),
        out_shape=jax.ShapeDtypeStruct((n_pad, d), F32),
        scratch_shapes=[pltpu.SemaphoreType.DMA],
        compiler_params=_params("arbitrary"),
        name="moe_dispatch",
    )(dest_tok.reshape(n // td, 1, td * TOP_K), h)


def _expert_kernel(be_ref, nu_ref, xs_ref, w1_ref, b1_ref, w2_ref, b2_ref, ys_ref, *, f):
    @pl.when(pl.program_id(0) < nu_ref[0])
    def _():
        gu = _dot(xs_ref[...].astype(BF16), w1_ref[0]) + b1_ref[0]
        g_h = jnp.minimum(gu[:, :f], SWIGLU_LIMIT)
        u_h = jnp.clip(gu[:, f:], -SWIGLU_LIMIT, SWIGLU_LIMIT)
        act = (u_h + 1.0) * (g_h * jax.nn.sigmoid(SWIGLU_ALPHA * g_h))
        ys_ref[...] = _dot(act.astype(BF16), w2_ref[0]) + b2_ref[0]


def _experts(xs, block_expert, n_used, w1, b1, w2, b2, eb):
    n_pad, d = xs.shape
    n_exp, _, f2 = w1.shape
    f = f2 // 2
    row = lambda i, be, nu: (jnp.minimum(i, nu[0] - 1), 0)
    grid_spec = pltpu.PrefetchScalarGridSpec(
        num_scalar_prefetch=2,
        grid=(n_pad // eb,),
        in_specs=[pl.BlockSpec((eb, d), row),
                  pl.BlockSpec((1, d, f2), lambda i, be, nu: (be[i], 0, 0)),
                  pl.BlockSpec((1, 1, f2), lambda i, be, nu: (be[i], 0, 0)),
                  pl.BlockSpec((1, f, d), lambda i, be, nu: (be[i], 0, 0)),
                  pl.BlockSpec((1, 1, d), lambda i, be, nu: (be[i], 0, 0))],
        out_specs=pl.BlockSpec((eb, d), row),
    )
    return pl.pallas_call(
        functools.partial(_expert_kernel, f=f),
        grid_spec=grid_spec,
        out_shape=jax.ShapeDtypeStruct((n_pad, d), F32),
        compiler_params=_params("arbitrary"),
        name="moe_experts",
    )(block_expert, n_used, xs, w1, b1.reshape(n_exp, 1, f2), w2, b2.reshape(n_exp, 1, d))


def _combine_kernel(dest_ref, x_ref, gate_ref, mod_ref, ys_ref, o_ref, buf, sem, *, tc):
    def issue(t, c):
        for k in range(TOP_K):
            slot = dest_ref[0, 0, t * TOP_K + k]
            pltpu.make_async_copy(ys_ref.at[pl.ds(slot, 1)], buf.at[k, pl.ds(t, 1)], sem).start()
        return c

    lax.fori_loop(0, tc, issue, 0)
    for k in range(TOP_K):
        pltpu.make_async_copy(ys_ref.at[pl.ds(0, tc)], buf.at[k], sem).wait()
    g = gate_ref[...]
    moe = (g[:, 0:1] * buf[0] + g[:, 1:2] * buf[1]) + (g[:, 2:3] * buf[2] + g[:, 3:4] * buf[3])
    o_ref[...] = x_ref[...] + mod_ref[0, 5:6, :] * moe


def _combine(x2, gates_tok, dest_tok, mod, ys, seq, tc):
    n, d = x2.shape
    per_b = seq // tc
    return pl.pallas_call(
        functools.partial(_combine_kernel, tc=tc),
        grid=(n // tc,),
        in_specs=[pl.BlockSpec((1, 1, tc * TOP_K), lambda i: (i, 0, 0), memory_space=pltpu.SMEM),
                  pl.BlockSpec((tc, d), lambda i: (i, 0)),
                  pl.BlockSpec((tc, TOP_K), lambda i: (i, 0)),
                  pl.BlockSpec((1, 6, d), lambda i: (i // per_b, 0, 0)),
                  pl.BlockSpec(memory_space=pl.ANY)],
        out_specs=pl.BlockSpec((tc, d), lambda i: (i, 0)),
        out_shape=jax.ShapeDtypeStruct((n, d), F32),
        scratch_shapes=[pltpu.VMEM((TOP_K, tc, d), F32), pltpu.SemaphoreType.DMA],
        compiler_params=_params("arbitrary"),
        name="moe_combine",
    )(dest_tok.reshape(n // tc, 1, tc * TOP_K), x2, gates_tok, mod, ys)


def _moe(x, mod, g2, router_w, router_b, w1, b1, w2, b2, tiles):
    b, s, d = x.shape
    n = b * s
    n_exp = router_w.shape[1]
    eb = tiles["eb"]
    x2 = x.reshape(n, d)
    h, idx, gates, rank, cnt = _router(x2, mod, g2, router_w, router_b, s, tiles["tr"])
    counts = cnt[:, 0]
    padded = (counts + eb - 1) // eb * eb
    pad_end = jnp.cumsum(padded)
    pad_start = pad_end - padded
    n_blocks = n * TOP_K // eb + n_exp
    dest_tok = (pad_start[idx] + rank).T.reshape(-1)
    block_expert = jnp.clip(
        jnp.searchsorted(pad_end, jnp.arange(n_blocks, dtype=jnp.int32) * eb, side="right"),
        0, n_exp - 1).astype(jnp.int32)
    n_used = (pad_end[-1:] // eb).astype(jnp.int32)
    xs = _dispatch(h, dest_tok, n_blocks * eb, tiles["td"])
    ys = _experts(xs, block_expert, n_used, w1, b1, w2, b2, eb)
    out = _combine(x2, gates.T, dest_tok, mod, ys, s, tiles["tc"])
    return out.reshape(b, s, d)


def _tiles(seq):
    return {"ts": min(seq, 512), "tq": min(seq, 256), "tr": min(seq, 512),
            "td": min(seq, 1024), "tc": min(seq, 256), "eb": 512}


def kernel(x, c, norm1_g, norm2_g, ada_w, ada_b, conv_w_in, conv_w, conv_w_out, attn_w_qkv, attn_q_g, attn_k_g, attn_w_o, router_w, router_b, moe_w1, moe_b1, moe_w2, moe_b2):
    depth = ada_w.shape[0]
    t = _tiles(x.shape[1])
    mods = _ada(c, ada_w, ada_b)
    for i in range(depth):
        mod = mods[i]
        g1 = norm1_g[i][None, :]
        j = i // 2
        if i % 2 == 0:
            x = _conv_mixer(x, mod, g1, conv_w_in[j].astype(BF16), conv_w[j],
                            conv_w_out[j].astype(BF16), t["ts"])
        else:
            q, k, v = _qkv(x, mod, g1, attn_w_qkv[j].astype(BF16), attn_q_g[j], attn_k_g[j], t["ts"])
            o = _attention(q, k, v, t["tq"])
            x = _oproj(x, o, mod, attn_w_o[j].astype(BF16), t["ts"])
        x = _moe(x, mod, norm2_g[i][None, :], router_w[i], router_b[i],
                 moe_w1[i].astype(BF16), moe_b1[i], moe_w2[i].astype(BF16), moe_b2[i], t)
    return x
```

```python
import functools

import jax
import jax.numpy as jnp
from jax import lax
from jax.experimental import pallas as pl
from jax.experimental.pallas import tpu as pltpu

HEAD_DIM = 64
TOP_K = 4
CONV_W = 3
EPS = 1e-6
SWIGLU_LIMIT = 7.0
SWIGLU_ALPHA = 1.702
LANES = 128
SUBLANES = 8
HEADS_PER_TILE = LANES // HEAD_DIM
NEG_CUTOFF = -106.0
VMEM_LIMIT = 56 * 1024 * 1024

F32 = jnp.float32
BF16 = jnp.bfloat16
_NT = (((1,), (1,)), ((), ()))


def _dot(a, b):
    return jnp.dot(a, b, preferred_element_type=F32)


def _split(a):
    hi = a.astype(BF16)
    lo = (a - hi.astype(F32)).astype(BF16)
    return hi, lo


def _modulate(x, g, sc, sh):
    ms = jnp.mean(x * x, axis=-1, keepdims=True)
    return (x * lax.rsqrt(ms + EPS)) * g * (1.0 + sc) + sh


def _params(*sem):
    return pltpu.CompilerParams(dimension_semantics=sem, vmem_limit_bytes=VMEM_LIMIT)


def _ada_kernel(c_ref, w_ref, b_ref, o_ref):
    c = c_ref[...]
    ca = c * jax.nn.sigmoid(c)
    o_ref[0] = jnp.dot(ca, w_ref[0], preferred_element_type=F32,
                       precision=lax.Precision.HIGHEST) + b_ref[0]


def _ada(c, ada_w, ada_b):
    depth, d, d6 = ada_w.shape
    b = c.shape[0]
    tn = min(d6, 1536)
    out = pl.pallas_call(
        _ada_kernel,
        grid=(depth, d6 // tn),
        in_specs=[pl.BlockSpec((b, d), lambda i, j: (0, 0)),
                  pl.BlockSpec((1, d, tn), lambda i, j: (i, 0, j)),
                  pl.BlockSpec((1, 1, tn), lambda i, j: (i, 0, j))],
        out_specs=pl.BlockSpec((1, b, tn), lambda i, j: (i, 0, j)),
        out_shape=jax.ShapeDtypeStruct((depth, b, d6), F32),
        compiler_params=_params("parallel", "parallel"),
        name="ada",
    )(c, ada_w, ada_b.reshape(depth, 1, d6))
    return out.reshape(depth, b, 6, d)


def _conv_kernel(x_ref, mod_ref, g_ref, win_ref, cw_ref, wout_ref, o_ref, carry_ref, *, ts, d):
    s = pl.program_id(1)

    @pl.when(s == 0)
    def _():
        carry_ref[...] = jnp.zeros_like(carry_ref)

    x = x_ref[0]
    h = _modulate(x, g_ref[...], mod_ref[0, 1:2, :], mod_ref[0, 0:1, :]).astype(BF16)
    proj = _dot(h, win_ref[...])
    v = proj[:, d:2 * d] * proj[:, 2 * d:]
    prev = carry_ref[...]
    row = lax.broadcasted_iota(jnp.int32, (ts, 1), 0)
    v1 = jnp.where(row == 0, prev[7:8, :], pltpu.roll(v, 1, axis=0))
    v2 = jnp.where(row == 0, prev[6:7, :],
                   jnp.where(row == 1, prev[7:8, :], pltpu.roll(v, 2, axis=0)))
    y = cw_ref[2:3, :] * v + cw_ref[1:2, :] * v1 + cw_ref[0:1, :] * v2
    carry_ref[...] = v[ts - 8:, :]
    out = _dot((proj[:, :d] * y).astype(BF16), wout_ref[...])
    o_ref[0] = x + mod_ref[0, 2:3, :] * out


def _conv_mixer(x, mod, g1, w_in, conv_w, w_out, ts):
    b, s, d = x.shape
    return pl.pallas_call(
        functools.partial(_conv_kernel, ts=ts, d=d),
        grid=(b, s // ts),
        in_specs=[pl.BlockSpec((1, ts, d), lambda i, j: (i, j, 0)),
                  pl.BlockSpec((1, 6, d), lambda i, j: (i, 0, 0)),
                  pl.BlockSpec((1, d), lambda i, j: (0, 0)),
                  pl.BlockSpec((d, 3 * d), lambda i, j: (0, 0)),
                  pl.BlockSpec((CONV_W, d), lambda i, j: (0, 0)),
                  pl.BlockSpec((d, d), lambda i, j: (0, 0))],
        out_specs=pl.BlockSpec((1, ts, d), lambda i, j: (i, j, 0)),
        out_shape=jax.ShapeDtypeStruct((b, s, d), F32),
        scratch_shapes=[pltpu.VMEM((8, d), F32)],
        compiler_params=_params("parallel", "arbitrary"),
        name="conv_mixer",
    )(x, mod, g1, w_in, conv_w, w_out)


def _head_rms(t, gsum_ref, gexp_ref):
    hi, lo = _split(t * t)
    ss = _dot(hi, gsum_ref[...]) + _dot(lo, gsum_ref[...])
    inv = lax.rsqrt(ss * (1.0 / HEAD_DIM) + EPS)
    ihi, ilo = _split(inv)
    return _dot(ihi, gexp_ref[...]) + _dot(ilo, gexp_ref[...])


def _qkv_kernel(x_ref, mod_ref, g_ref, w_ref, qg_ref, kg_ref, gsum_ref, gexp_ref,
                q_ref, k_ref, v_ref, *, d):
    x = x_ref[0]
    h = _modulate(x, g_ref[...], mod_ref[0, 1:2, :], mod_ref[0, 0:1, :]).astype(BF16)
    qkv = _dot(h, w_ref[...])
    q = qkv[:, :d]
    k = qkv[:, d:2 * d]
    q_ref[0] = (q * _head_rms(q, gsum_ref, gexp_ref) * (qg_ref[...] * HEAD_DIM ** -0.5)).astype(BF16)
    k_ref[0] = (k * _head_rms(k, gsum_ref, gexp_ref) * kg_ref[...]).astype(BF16)
    v_ref[0] = qkv[:, 2 * d:].astype(BF16)


def _qkv(x, mod, g1, w_qkv, q_g, k_g, ts):
    b, s, d = x.shape
    n_heads = d // HEAD_DIM
    head_of = jnp.arange(d) // HEAD_DIM
    gsum = (head_of[:, None] == jnp.arange(LANES)[None, :]).astype(BF16)
    gexp = gsum.T
    qg = jnp.tile(q_g, n_heads)[None, :]
    kg = jnp.tile(k_g, n_heads)[None, :]
    blk = pl.BlockSpec((1, ts, d), lambda i, j: (i, j, 0))
    full = lambda shape: pl.BlockSpec(shape, lambda i, j: (0,) * len(shape))
    return pl.pallas_call(
        functools.partial(_qkv_kernel, d=d),
        grid=(b, s // ts),
        in_specs=[blk, pl.BlockSpec((1, 6, d), lambda i, j: (i, 0, 0)), full((1, d)),
                  full((d, 3 * d)), full((1, d)), full((1, d)), full((d, LANES)), full((LANES, d))],
        out_specs=[blk, blk, blk],
        out_shape=[jax.ShapeDtypeStruct((b, s, d), BF16)] * 3,
        compiler_params=_params("parallel", "parallel"),
        name="qkv",
    )(x, mod, g1, w_qkv, qg, kg, gsum, gexp)


def _attn_kernel(q_ref, k_ref, v_ref, t_ref, o_ref, *, tq):
    qi = pl.program_id(2)
    q = q_ref[0]
    lane = lax.broadcasted_iota(jnp.int32, (1, LANES), 1)
    row = lax.broadcasted_iota(jnp.int32, (tq, tq), 0)
    col = lax.broadcasted_iota(jnp.int32, (tq, tq), 1)
    causal = col < row
    in_head = [(lane // HEAD_DIM) == j for j in range(HEADS_PER_TILE)]
    qs = [jnp.where(m, q, jnp.zeros_like(q)) for m in in_head]

    def block(kb, qj, carry, masked):
        start = pl.multiple_of(kb * tq, tq)
        kblk = k_ref[0, pl.ds(start, tq), :]
        vblk = v_ref[0, pl.ds(start, tq), :]
        z = lax.dot_general(qj, kblk, _NT, preferred_element_type=F32)
        log_beta = jnp.minimum(z, 0.0) - jnp.log(1.0 + jnp.exp(-jnp.abs(z)))
        l1m = log_beta - z
        if masked:
            l1m = jnp.where(causal, l1m, 0.0)
        suf = _dot(l1m.astype(BF16), t_ref[...])
        expo = log_beta + suf
        if carry is not None:
            expo = expo + carry
        a = jnp.exp(expo)
        if masked:
            a = jnp.where(causal, a, 0.0)
        return _dot(a.astype(BF16), vblk), suf[:, 0:1] + l1m[:, 0:1]

    k_prev = jnp.maximum(qi - 1, 0)
    has_prev = (qi > 0).astype(F32)
    state = []
    for qj in qs:
        o_d, tot_d = block(qi, qj, None, True)
        o_p, tot_p = block(k_prev, qj, tot_d, False)
        state += [tot_d + has_prev * tot_p, o_d + has_prev * o_p]

    def cond(st):
        kb, c0, _, c1, _ = st
        return jnp.logical_and(kb >= 0, jnp.maximum(jnp.max(c0), jnp.max(c1)) > NEG_CUTOFF)

    def body(st):
        kb, c0, a0, c1, a1 = st
        o0, t0 = block(kb, qs[0], c0, False)
        o1, t1 = block(kb, qs[1], c1, False)
        return kb - 1, c0 + t0, a0 + o0, c1 + t1, a1 + o1

    _, _, a0, _, a1 = lax.while_loop(cond, body, (qi - 2, *state))
    o_ref[0] = jnp.where(in_head[0], a0, a1).astype(o_ref.dtype)


def _attention(q, k, v, tq):
    b, s, d = q.shape
    jj = jnp.arange(tq)
    text = (jj[:, None] > jj[None, :]).astype(BF16)
    return pl.pallas_call(
        functools.partial(_attn_kernel, tq=tq),
        grid=(b, d // LANES, s // tq),
        in_specs=[pl.BlockSpec((1, tq, LANES), lambda i, h, j: (i, j, h)),
                  pl.BlockSpec((1, s, LANES), lambda i, h, j: (i, 0, h)),
                  pl.BlockSpec((1, s, LANES), lambda i, h, j: (i, 0, h)),
                  pl.BlockSpec((tq, tq), lambda i, h, j: (0, 0))],
        out_specs=pl.BlockSpec((1, tq, LANES), lambda i, h, j: (i, j, h)),
        out_shape=jax.ShapeDtypeStruct((b, s, d), BF16),
        compiler_params=_params("parallel", "parallel", "parallel"),
        name="stickbreak_attn",
    )(q, k, v, text)


def _oproj_kernel(x_ref, o_ref_in, mod_ref, w_ref, out_ref):
    out_ref[0] = x_ref[0] + mod_ref[0, 2:3, :] * _dot(o_ref_in[0], w_ref[...])


def _oproj(x, o, mod, w_o, ts):
    b, s, d = x.shape
    blk = pl.BlockSpec((1, ts, d), lambda i, j: (i, j, 0))
    return pl.pallas_call(
        _oproj_kernel,
        grid=(b, s // ts),
        in_specs=[blk, blk, pl.BlockSpec((1, 6, d), lambda i, j: (i, 0, 0)),
                  pl.BlockSpec((d, d), lambda i, j: (0, 0))],
        out_specs=blk,
        out_shape=jax.ShapeDtypeStruct((b, s, d), F32),
        compiler_params=_params("parallel", "parallel"),
        name="attn_oproj",
    )(x, o, mod, w_o)


def _router_kernel(x_ref, mod_ref, g_ref, rwh_ref, rwl_ref, rb_ref, tri_ref,
                   h_ref, idx_ref, gate_ref, rank_ref, cnt_ref, base_ref, *, n_exp, tr):
    @pl.when(pl.program_id(0) == 0)
    def _():
        base_ref[...] = jnp.zeros_like(base_ref)

    h = _modulate(x_ref[...], g_ref[...], mod_ref[0, 4:5, :], mod_ref[0, 3:4, :])
    h_ref[...] = h
    hh, hl = _split(h)
    rwh = rwh_ref[...]
    logits = (lax.dot_general(rwh, hh, _NT, preferred_element_type=F32)
              + lax.dot_general(rwh, hl, _NT, preferred_element_type=F32)
              + lax.dot_general(rwl_ref[...], hh, _NT, preferred_element_type=F32)) + rb_ref[...]
    eio = lax.broadcasted_iota(jnp.int32, (n_exp, tr), 0)
    work = logits
    tops, onehots = [], []
    for k in range(TOP_K):
        m = jnp.max(work, axis=0, keepdims=True)
        am = jnp.min(jnp.where(work == m, eio, n_exp), axis=0, keepdims=True)
        oh = eio == am
        work = jnp.where(oh, -jnp.inf, work)
        tops.append(m)
        onehots.append(oh)
        idx_ref[k:k + 1, :] = am
    es = [jnp.exp(m - tops[0]) for m in tops]
    den = (es[0] + es[1]) + (es[2] + es[3])
    for k in range(TOP_K):
        gate_ref[k:k + 1, :] = es[k] / den
    sel = jnp.zeros((n_exp, tr), F32)
    for oh in onehots:
        sel = sel + oh.astype(F32)
    cum = _dot(sel.astype(BF16), tri_ref[...])
    base = base_ref[...]
    pos = cum[:, :tr] + jnp.concatenate([base] * (tr // LANES), axis=1)
    for k in range(TOP_K):
        rank_ref[k:k + 1, :] = jnp.sum(jnp.where(onehots[k], pos, 0.0), axis=0,
                                       keepdims=True).astype(jnp.int32)
    base = base + cum[:, tr:]
    base_ref[...] = base
    cnt_ref[...] = base.astype(jnp.int32)


def _router(x2, mod, g2, router_w, router_b, seq, tr):
    n, d = x2.shape
    n_exp = router_w.shape[1]
    rwh, rwl = _split(router_w.T)
    jj = jnp.arange(tr)
    tri = (jj[:, None] < jj[None, :]).astype(BF16)
    text = jnp.concatenate([tri, jnp.ones((tr, LANES), BF16)], axis=1)
    per_b = seq // tr
    full = lambda shape: pl.BlockSpec(shape, lambda i: (0,) * len(shape))
    row4 = pl.BlockSpec((TOP_K, tr), lambda i: (0, i))
    return pl.pallas_call(
        functools.partial(_router_kernel, n_exp=n_exp, tr=tr),
        grid=(n // tr,),
        in_specs=[pl.BlockSpec((tr, d), lambda i: (i, 0)),
                  pl.BlockSpec((1, 6, d), lambda i: (i // per_b, 0, 0)),
                  full((1, d)), full((n_exp, d)), full((n_exp, d)), full((n_exp, 1)),
                  full((tr, tr + LANES))],
        out_specs=[pl.BlockSpec((tr, d), lambda i: (i, 0)), row4, row4, row4,
                   full((n_exp, LANES))],
        out_shape=[jax.ShapeDtypeStruct((n, d), F32),
                   jax.ShapeDtypeStruct((TOP_K, n), jnp.int32),
                   jax.ShapeDtypeStruct((TOP_K, n), F32),
                   jax.ShapeDtypeStruct((TOP_K, n), jnp.int32),
                   jax.ShapeDtypeStruct((n_exp, LANES), jnp.int32)],
        scratch_shapes=[pltpu.VMEM((n_exp, LANES), F32)],
        compiler_params=_params("arbitrary"),
        name="moe_router",
    )(x2, mod, g2, rwh, rwl, router_b[:, None], text)


def _zero_fill(fs_ref, fl_ref, nu_ref, zero_ref, xs_ref, zsem, *, eb, n_exp, n_blocks, wait):
    def go(copy):
        if wait:
            copy.wait()
        else:
            copy.start()

    def per_expert(e, c):
        first = fs_ref[e]
        length = fl_ref[e]
        end = first + length
        for bit in reversed(range(SUBLANES.bit_length() - 1, eb.bit_length() - 1)):
            size = 1 << bit
            end = end - (length & size)

            @pl.when((length & size) != 0)
            def _(off=end, size=size):
                off = pl.multiple_of(off, SUBLANES)
                go(pltpu.make_async_copy(zero_ref.at[pl.ds(0, size)], xs_ref.at[pl.ds(off, size)], zsem))

        for r in range(SUBLANES - 1):
            @pl.when(r < (length & (SUBLANES - 1)))
            def _(r=r):
                go(pltpu.make_async_copy(zero_ref.at[pl.ds(0, 1)], xs_ref.at[pl.ds(first + r, 1)], zsem))

        return c

    lax.fori_loop(0, n_exp, per_expert, 0)

    def per_tail(j, c):
        blk = nu_ref[0] + j

        @pl.when(blk < n_blocks)
        def _():
            go(pltpu.make_async_copy(zero_ref, xs_ref.at[pl.ds(pl.multiple_of(blk * eb, eb), eb)], zsem))

        return c

    lax.fori_loop(0, n_exp, per_tail, 0)


def _dispatch_kernel(fs_ref, fl_ref, nu_ref, dest_ref, h_ref, xs_ref, zero_ref, sem, zsem,
                     *, td, eb, n_exp, n_blocks):
    i = pl.program_id(0)
    fill = functools.partial(_zero_fill, fs_ref, fl_ref, nu_ref, zero_ref, xs_ref, zsem,
                             eb=eb, n_exp=n_exp, n_blocks=n_blocks)

    @pl.when(i == 0)
    def _():
        zero_ref[...] = jnp.zeros_like(zero_ref)
        fill(wait=False)

    def issue(t, c):
        for k in range(TOP_K):
            slot = dest_ref[0, 0, t * TOP_K + k]
            pltpu.make_async_copy(h_ref.at[pl.ds(t, 1)], xs_ref.at[pl.ds(slot, 1)], sem).start()
        return c

    lax.fori_loop(0, td, issue, 0)
    for _ in range(TOP_K):
        pltpu.make_async_copy(h_ref, xs_ref.at[pl.ds(0, td)], sem).wait()

    @pl.when(i == pl.num_programs(0) - 1)
    def _():
        fill(wait=True)


def _dispatch(h, dest_tok, fill_start, fill_len, n_used, n_blocks, eb, td):
    n, d = h.shape
    n_exp = fill_start.shape[0]
    grid_spec = pltpu.PrefetchScalarGridSpec(
        num_scalar_prefetch=3,
        grid=(n // td,),
        in_specs=[pl.BlockSpec((1, 1, td * TOP_K), lambda i, *_: (i, 0, 0), memory_space=pltpu.SMEM),
                  pl.BlockSpec((td, d), lambda i, *_: (i, 0))],
        out_specs=pl.BlockSpec(memory_space=pl.ANY),
        scratch_shapes=[pltpu.VMEM((eb, d), F32), pltpu.SemaphoreType.DMA, pltpu.SemaphoreType.DMA],
    )
    return pl.pallas_call(
        functools.partial(_dispatch_kernel, td=td, eb=eb, n_exp=n_exp, n_blocks=n_blocks),
        grid_spec=grid_spec,
        out_shape=jax.ShapeDtypeStruct((n_blocks * eb, d), F32),
        compiler_params=_params("arbitrary"),
        name="moe_dispatch",
    )(fill_start, fill_len, n_used, dest_tok.reshape(n // td, 1, td * TOP_K), h)


def _expert_kernel(be_ref, nu_ref, xs_ref, w1_ref, b1_ref, w2_ref, b2_ref, ys_ref, *, f):
    used = pl.program_id(0) < nu_ref[0]

    @pl.when(used)
    def _():
        gu = _dot(xs_ref[...].astype(BF16), w1_ref[0]) + b1_ref[0]
        g_h = jnp.minimum(gu[:, :f], SWIGLU_LIMIT)
        u_h = jnp.clip(gu[:, f:], -SWIGLU_LIMIT, SWIGLU_LIMIT)
        act = (u_h + 1.0) * (g_h * jax.nn.sigmoid(SWIGLU_ALPHA * g_h))
        ys_ref[...] = _dot(act.astype(BF16), w2_ref[0]) + b2_ref[0]

    @pl.when(jnp.logical_not(used))
    def _():
        ys_ref[...] = jnp.zeros_like(ys_ref)


def _experts(xs, block_expert, n_used, w1, b1, w2, b2, eb):
    n_pad, d = xs.shape
    n_exp, _, f2 = w1.shape
    f = f2 // 2
    grid_spec = pltpu.PrefetchScalarGridSpec(
        num_scalar_prefetch=2,
        grid=(n_pad // eb,),
        in_specs=[pl.BlockSpec((eb, d), lambda i, be, nu: (jnp.minimum(i, nu[0] - 1), 0)),
                  pl.BlockSpec((1, d, f2), lambda i, be, nu: (be[i], 0, 0)),
                  pl.BlockSpec((1, 1, f2), lambda i, be, nu: (be[i], 0, 0)),
                  pl.BlockSpec((1, f, d), lambda i, be, nu: (be[i], 0, 0)),
                  pl.BlockSpec((1, 1, d), lambda i, be, nu: (be[i], 0, 0))],
        out_specs=pl.BlockSpec((eb, d), lambda i, be, nu: (i, 0)),
    )
    return pl.pallas_call(
        functools.partial(_expert_kernel, f=f),
        grid_spec=grid_spec,
        out_shape=jax.ShapeDtypeStruct((n_pad, d), F32),
        compiler_params=_params("arbitrary"),
        name="moe_experts",
    )(block_expert, n_used, xs, w1, b1.reshape(n_exp, 1, f2), w2, b2.reshape(n_exp, 1, d))


def _combine_kernel(dest_ref, x_ref, gate_ref, mod_ref, ys_ref, o_ref, buf, sem, *, tc):
    def issue(t, c):
        for k in range(TOP_K):
            slot = dest_ref[0, 0, t * TOP_K + k]
            pltpu.make_async_copy(ys_ref.at[pl.ds(slot, 1)], buf.at[k, pl.ds(t, 1)], sem).start()
        return c

    lax.fori_loop(0, tc, issue, 0)
    for k in range(TOP_K):
        pltpu.make_async_copy(ys_ref.at[pl.ds(0, tc)], buf.at[k], sem).wait()
    g = gate_ref[...]
    moe = (g[:, 0:1] * buf[0] + g[:, 1:2] * buf[1]) + (g[:, 2:3] * buf[2] + g[:, 3:4] * buf[3])
    o_ref[...] = x_ref[...] + mod_ref[0, 5:6, :] * moe


def _combine(x2, gates_tok, dest_tok, mod, ys, seq, tc):
    n, d = x2.shape
    per_b = seq // tc
    return pl.pallas_call(
        functools.partial(_combine_kernel, tc=tc),
        grid=(n // tc,),
        in_specs=[pl.BlockSpec((1, 1, tc * TOP_K), lambda i: (i, 0, 0), memory_space=pltpu.SMEM),
                  pl.BlockSpec((tc, d), lambda i: (i, 0)),
                  pl.BlockSpec((tc, TOP_K), lambda i: (i, 0)),
                  pl.BlockSpec((1, 6, d), lambda i: (i // per_b, 0, 0)),
                  pl.BlockSpec(memory_space=pl.ANY)],
        out_specs=pl.BlockSpec((tc, d), lambda i: (i, 0)),
        out_shape=jax.ShapeDtypeStruct((n, d), F32),
        scratch_shapes=[pltpu.VMEM((TOP_K, tc, d), F32), pltpu.SemaphoreType.DMA],
        compiler_params=_params("arbitrary"),
        name="moe_combine",
    )(dest_tok.reshape(n // tc, 1, tc * TOP_K), x2, gates_tok, mod, ys)


def _moe(x, mod, g2, router_w, router_b, w1, b1, w2, b2, tiles):
    b, s, d = x.shape
    n = b * s
    n_exp = router_w.shape[1]
    eb = tiles["eb"]
    x2 = x.reshape(n, d)
    h, idx, gates, rank, cnt = _router(x2, mod, g2, router_w, router_b, s, tiles["tr"])
    counts = cnt[:, 0]
    padded = (counts + eb - 1) // eb * eb
    pad_end = jnp.cumsum(padded)
    pad_start = pad_end - padded
    n_blocks = n * TOP_K // eb + n_exp
    is_e = idx[None] == jnp.arange(n_exp, dtype=jnp.int32)[:, None, None]
    dest = rank + jnp.sum(jnp.where(is_e, pad_start[:, None, None], 0), axis=0)
    dest_tok = dest.T.reshape(-1)
    block_row0 = jnp.arange(n_blocks, dtype=jnp.int32) * eb
    block_expert = jnp.minimum(jnp.sum(pad_end[None, :] <= block_row0[:, None], axis=1),
                               n_exp - 1).astype(jnp.int32)
    n_used = (pad_end[-1:] // eb).astype(jnp.int32)
    fill_start = pad_start + counts
    xs = _dispatch(h, dest_tok, fill_start, pad_end - fill_start, n_used, n_blocks, eb, tiles["td"])
    ys = _experts(xs, block_expert, n_used, w1, b1, w2, b2, eb)
    out = _combine(x2, gates.T, dest_tok, mod, ys, s, tiles["tc"])
    return out.reshape(b, s, d)


def _tiles(seq):
    return {"ts": min(seq, 512), "tq": min(seq, 256), "tr": min(seq, 512),
            "td": min(seq, 1024), "tc": min(seq, 256), "eb": 512}


def kernel(x, c, norm1_g, norm2_g, ada_w, ada_b, conv_w_in, conv_w, conv_w_out, attn_w_qkv, attn_q_g, attn_k_g, attn_w_o, router_w, router_b, moe_w1, moe_b1, moe_w2, moe_b2):
    depth = ada_w.shape[0]
    t = _tiles(x.shape[1])
    mods = _ada(c, ada_w, ada_b)
    for i in range(depth):
        mod = mods[i]
        g1 = norm1_g[i][None, :]
        j = i // 2
        if i % 2 == 0:
            x = _conv_mixer(x, mod, g1, conv_w_in[j].astype(BF16), conv_w[j],
                            conv_w_out[j].astype(BF16), t["ts"])
        else:
            q, k, v = _qkv(x, mod, g1, attn_w_qkv[j].astype(BF16), attn_q_g[j], attn_k_g[j], t["ts"])
            o = _attention(q, k, v, t["tq"])
            x = _oproj(x, o, mod, attn_w_o[j].astype(BF16), t["ts"])
        x = _moe(x, mod, norm2_g[i][None, :], router_w[i], router_b[i],
                 moe_w1[i].astype(BF16), moe_b1[i], moe_w2[i].astype(BF16), moe_b2[i], t)
    return x
```

```python
import functools

import jax
import jax.numpy as jnp
from jax import lax
from jax.experimental import pallas as pl
from jax.experimental.pallas import tpu as pltpu

HEAD_DIM = 64
TOP_K = 4
CONV_W = 3
EPS = 1e-6
SWIGLU_LIMIT = 7.0
SWIGLU_ALPHA = 1.702
LANES = 128
SUBLANES = 8
HEADS_PER_TILE = LANES // HEAD_DIM
NEG_CUTOFF = -106.0
VMEM_LIMIT = 56 * 1024 * 1024
ISSUE_UNROLL = 4

F32 = jnp.float32
BF16 = jnp.bfloat16
_NT = (((1,), (1,)), ((), ()))


def _dot(a, b):
    return jnp.dot(a, b, preferred_element_type=F32)


def _split(a):
    hi = a.astype(BF16)
    lo = (a - hi.astype(F32)).astype(BF16)
    return hi, lo


def _modulate(x, g, sc, sh):
    ms = jnp.mean(x * x, axis=-1, keepdims=True)
    return (x * lax.rsqrt(ms + EPS)) * g * (1.0 + sc) + sh


def _load_rows(ref, n_rows, n_chunks, lead=()):
    return [ref[(*lead, pl.ds(c, n_rows, stride=n_chunks), slice(None))] for c in range(n_chunks)]


def _store_rows(ref, value, n_rows, n_chunks):
    for c in range(n_chunks):
        ref[pl.ds(c, n_rows, stride=n_chunks), :] = value[:, c * LANES:(c + 1) * LANES]


def _params(*sem):
    return pltpu.CompilerParams(dimension_semantics=sem, vmem_limit_bytes=VMEM_LIMIT)


def _ada_kernel(c_ref, w_ref, b_ref, o_ref):
    c = c_ref[...]
    ca = c * jax.nn.sigmoid(c)
    o_ref[0] = jnp.dot(ca, w_ref[0], preferred_element_type=F32,
                       precision=lax.Precision.HIGHEST) + b_ref[0]


def _ada(c, ada_w, ada_b):
    depth, d, d6 = ada_w.shape
    b = c.shape[0]
    tn = min(d6, 1536)
    out = pl.pallas_call(
        _ada_kernel,
        grid=(depth, d6 // tn),
        in_specs=[pl.BlockSpec((b, d), lambda i, j: (0, 0)),
                  pl.BlockSpec((1, d, tn), lambda i, j: (i, 0, j)),
                  pl.BlockSpec((1, 1, tn), lambda i, j: (i, 0, j))],
        out_specs=pl.BlockSpec((1, b, tn), lambda i, j: (i, 0, j)),
        out_shape=jax.ShapeDtypeStruct((depth, b, d6), F32),
        compiler_params=_params("parallel", "parallel"),
        name="ada",
    )(c, ada_w, ada_b.reshape(depth, 1, d6))
    return out.reshape(depth, b, 6, d)


def _conv_kernel(x_ref, mod_ref, g_ref, win_ref, cw_ref, wout_ref, o_ref, carry_ref, *, ts, d):
    s = pl.program_id(1)

    @pl.when(s == 0)
    def _():
        carry_ref[...] = jnp.zeros_like(carry_ref)

    x = x_ref[0]
    h = _modulate(x, g_ref[...], mod_ref[0, 1:2, :], mod_ref[0, 0:1, :]).astype(BF16)
    proj = _dot(h, win_ref[...])
    v = proj[:, d:2 * d] * proj[:, 2 * d:]
    prev = carry_ref[...]
    row = lax.broadcasted_iota(jnp.int32, (ts, 1), 0)
    v1 = jnp.where(row == 0, prev[7:8, :], pltpu.roll(v, 1, axis=0))
    v2 = jnp.where(row == 0, prev[6:7, :],
                   jnp.where(row == 1, prev[7:8, :], pltpu.roll(v, 2, axis=0)))
    y = cw_ref[2:3, :] * v + cw_ref[1:2, :] * v1 + cw_ref[0:1, :] * v2
    carry_ref[...] = v[ts - 8:, :]
    out = _dot((proj[:, :d] * y).astype(BF16), wout_ref[...])
    o_ref[0] = x + mod_ref[0, 2:3, :] * out


def _conv_mixer(x, mod, g1, w_in, conv_w, w_out, ts):
    b, s, d = x.shape
    return pl.pallas_call(
        functools.partial(_conv_kernel, ts=ts, d=d),
        grid=(b, s // ts),
        in_specs=[pl.BlockSpec((1, ts, d), lambda i, j: (i, j, 0)),
                  pl.BlockSpec((1, 6, d), lambda i, j: (i, 0, 0)),
                  pl.BlockSpec((1, d), lambda i, j: (0, 0)),
                  pl.BlockSpec((d, 3 * d), lambda i, j: (0, 0)),
                  pl.BlockSpec((CONV_W, d), lambda i, j: (0, 0)),
                  pl.BlockSpec((d, d), lambda i, j: (0, 0))],
        out_specs=pl.BlockSpec((1, ts, d), lambda i, j: (i, j, 0)),
        out_shape=jax.ShapeDtypeStruct((b, s, d), F32),
        scratch_shapes=[pltpu.VMEM((8, d), F32)],
        compiler_params=_params("parallel", "arbitrary"),
        name="conv_mixer",
    )(x, mod, g1, w_in, conv_w, w_out)


def _head_rms(t, gsum_ref, gexp_ref):
    ss = _dot((t * t).astype(BF16), gsum_ref[...])
    inv = lax.rsqrt(ss * (1.0 / HEAD_DIM) + EPS)
    ihi, ilo = _split(inv)
    return _dot(ihi, gexp_ref[...]) + _dot(ilo, gexp_ref[...])


def _qkv_kernel(x_ref, mod_ref, g_ref, w_ref, qg_ref, kg_ref, gsum_ref, gexp_ref,
                q_ref, k_ref, v_ref, *, d):
    x = x_ref[0]
    h = _modulate(x, g_ref[...], mod_ref[0, 1:2, :], mod_ref[0, 0:1, :]).astype(BF16)
    qkv = _dot(h, w_ref[...])
    q = qkv[:, :d]
    k = qkv[:, d:2 * d]
    q_ref[0] = (q * _head_rms(q, gsum_ref, gexp_ref) * (qg_ref[...] * HEAD_DIM ** -0.5)).astype(BF16)
    k_ref[0] = (k * _head_rms(k, gsum_ref, gexp_ref) * kg_ref[...]).astype(BF16)
    v_ref[0] = qkv[:, 2 * d:].astype(BF16)


def _qkv(x, mod, g1, w_qkv, q_g, k_g, ts):
    b, s, d = x.shape
    n_heads = d // HEAD_DIM
    head_of = jnp.arange(d) // HEAD_DIM
    gsum = (head_of[:, None] == jnp.arange(LANES)[None, :]).astype(BF16)
    gexp = gsum.T
    qg = jnp.tile(q_g, n_heads)[None, :]
    kg = jnp.tile(k_g, n_heads)[None, :]
    blk = pl.BlockSpec((1, ts, d), lambda i, j: (i, j, 0))
    full = lambda shape: pl.BlockSpec(shape, lambda i, j: (0,) * len(shape))
    return pl.pallas_call(
        functools.partial(_qkv_kernel, d=d),
        grid=(b, s // ts),
        in_specs=[blk, pl.BlockSpec((1, 6, d), lambda i, j: (i, 0, 0)), full((1, d)),
                  full((d, 3 * d)), full((1, d)), full((1, d)), full((d, LANES)), full((LANES, d))],
        out_specs=[blk, blk, blk],
        out_shape=[jax.ShapeDtypeStruct((b, s, d), BF16)] * 3,
        compiler_params=_params("parallel", "parallel"),
        name="qkv",
    )(x, mod, g1, w_qkv, qg, kg, gsum, gexp)


def _attn_kernel(q_ref, k_ref, v_ref, t_ref, o_ref, *, tq):
    qi = pl.program_id(2)
    q = q_ref[0]
    first_head = lax.broadcasted_iota(jnp.int32, (1, LANES), 1) < HEAD_DIM
    zero = jnp.zeros_like(q)
    q2 = jnp.concatenate([jnp.where(first_head, q, zero), jnp.where(first_head, zero, q)], axis=0)
    rows = HEADS_PER_TILE * tq
    row = lax.broadcasted_iota(jnp.int32, (rows, tq), 0)
    col = lax.broadcasted_iota(jnp.int32, (rows, tq), 1)
    causal = col < jnp.where(row >= tq, row - tq, row)

    def scores(kb):
        start = pl.multiple_of(kb * tq, tq)
        z = lax.dot_general(q2, k_ref[0, pl.ds(start, tq), :], _NT, preferred_element_type=F32)
        return z, v_ref[0, pl.ds(start, tq), :]

    def logs(z, masked):
        log_beta = jnp.minimum(z, 0.0) - jnp.log(1.0 + jnp.exp(-jnp.abs(z)))
        l1m = log_beta - z
        return log_beta, (jnp.where(causal, l1m, 0.0) if masked else l1m)

    def row_total(suf, l1m):
        return suf[:, 0:1] + l1m[:, 0:1]

    z_d, v_d = scores(qi)
    z_p, v_p = scores(jnp.maximum(qi - 1, 0))
    has_prev = (qi > 0).astype(F32)
    lb_d, lm_d = logs(z_d, True)
    lb_p, lm_p = logs(z_p, False)
    suf = _dot(jnp.concatenate([lm_d, lm_p], axis=0).astype(BF16), t_ref[...])
    suf_d, suf_p = suf[:rows], suf[rows:]
    tot_d = row_total(suf_d, lm_d)
    a_d = jnp.where(causal, jnp.exp(lb_d + suf_d), 0.0)
    a_p = jnp.exp(lb_p + suf_p + tot_d)
    acc = _dot(a_d.astype(BF16), v_d) + has_prev * _dot(a_p.astype(BF16), v_p)
    carry = tot_d + has_prev * row_total(suf_p, lm_p)

    def cond(st):
        kb, carry, _ = st
        return jnp.logical_and(kb >= 0, jnp.max(carry) > NEG_CUTOFF)

    def body(st):
        kb, carry, acc = st
        z, vblk = scores(kb)
        log_beta, l1m = logs(z, False)
        suf = _dot(l1m.astype(BF16), t_ref[...])
        a = jnp.exp(log_beta + suf + carry)
        return kb - 1, carry + row_total(suf, l1m), acc + _dot(a.astype(BF16), vblk)

    _, _, acc = lax.while_loop(cond, body, (qi - 2, carry, acc))
    o_ref[0] = jnp.where(first_head, acc[:tq], acc[tq:]).astype(o_ref.dtype)


def _attention(q, k, v, tq):
    b, s, d = q.shape
    jj = jnp.arange(tq)
    text = (jj[:, None] > jj[None, :]).astype(BF16)
    return pl.pallas_call(
        functools.partial(_attn_kernel, tq=tq),
        grid=(b, d // LANES, s // tq),
        in_specs=[pl.BlockSpec((1, tq, LANES), lambda i, h, j: (i, j, h)),
                  pl.BlockSpec((1, s, LANES), lambda i, h, j: (i, 0, h)),
                  pl.BlockSpec((1, s, LANES), lambda i, h, j: (i, 0, h)),
                  pl.BlockSpec((tq, tq), lambda i, h, j: (0, 0))],
        out_specs=pl.BlockSpec((1, tq, LANES), lambda i, h, j: (i, j, h)),
        out_shape=jax.ShapeDtypeStruct((b, s, d), BF16),
        compiler_params=_params("parallel", "parallel", "parallel"),
        name="stickbreak_attn",
    )(q, k, v, text)


def _oproj_kernel(x_ref, o_ref_in, mod_ref, w_ref, out_ref):
    out_ref[0] = x_ref[0] + mod_ref[0, 2:3, :] * _dot(o_ref_in[0], w_ref[...])


def _oproj(x, o, mod, w_o, ts):
    b, s, d = x.shape
    blk = pl.BlockSpec((1, ts, d), lambda i, j: (i, j, 0))
    return pl.pallas_call(
        _oproj_kernel,
        grid=(b, s // ts),
        in_specs=[blk, blk, pl.BlockSpec((1, 6, d), lambda i, j: (i, 0, 0)),
                  pl.BlockSpec((d, d), lambda i, j: (0, 0))],
        out_specs=blk,
        out_shape=jax.ShapeDtypeStruct((b, s, d), F32),
        compiler_params=_params("parallel", "parallel"),
        name="attn_oproj",
    )(x, o, mod, w_o)


def _router_kernel(x_ref, mod_ref, g_ref, rwh_ref, rwl_ref, rb_ref, tri_ref,
                   h_ref, idx_ref, gate_ref, rank_ref, cnt_ref, base_ref, *, n_exp, tr):
    @pl.when(pl.program_id(0) == 0)
    def _():
        base_ref[...] = jnp.zeros_like(base_ref)

    h = _modulate(x_ref[...], g_ref[...], mod_ref[0, 4:5, :], mod_ref[0, 3:4, :])
    _store_rows(h_ref, h, tr, h.shape[1] // LANES)
    hh, hl = _split(h)
    rwh = rwh_ref[...]
    logits = (lax.dot_general(rwh, hh, _NT, preferred_element_type=F32)
              + lax.dot_general(rwh, hl, _NT, preferred_element_type=F32)
              + lax.dot_general(rwl_ref[...], hh, _NT, preferred_element_type=F32)) + rb_ref[...]
    eio = lax.broadcasted_iota(jnp.int32, (n_exp, tr), 0)
    work = logits
    tops, onehots = [], []
    for k in range(TOP_K):
        m = jnp.max(work, axis=0, keepdims=True)
        am = jnp.min(jnp.where(work == m, eio, n_exp), axis=0, keepdims=True)
        oh = eio == am
        work = jnp.where(oh, -jnp.inf, work)
        tops.append(m)
        onehots.append(oh)
        idx_ref[k:k + 1, :] = am
    es = [jnp.exp(m - tops[0]) for m in tops]
    den = (es[0] + es[1]) + (es[2] + es[3])
    for k in range(TOP_K):
        gate_ref[k:k + 1, :] = es[k] / den
    sel = jnp.zeros((n_exp, tr), F32)
    for oh in onehots:
        sel = sel + oh.astype(F32)
    cum = _dot(sel.astype(BF16), tri_ref[...])
    base = base_ref[...]
    pos = cum[:, :tr] + jnp.concatenate([base] * (tr // LANES), axis=1)
    for k in range(TOP_K):
        rank_ref[k:k + 1, :] = jnp.sum(jnp.where(onehots[k], pos, 0.0), axis=0,
                                       keepdims=True).astype(jnp.int32)
    base = base + cum[:, tr:]
    base_ref[...] = base
    cnt_ref[...] = base.astype(jnp.int32)


def _router(x2, mod, g2, router_w, router_b, seq, tr):
    n, d = x2.shape
    n_exp = router_w.shape[1]
    rwh, rwl = _split(router_w.T)
    jj = jnp.arange(tr)
    tri = (jj[:, None] < jj[None, :]).astype(BF16)
    text = jnp.concatenate([tri, jnp.ones((tr, LANES), BF16)], axis=1)
    per_b = seq // tr
    full = lambda shape: pl.BlockSpec(shape, lambda i: (0,) * len(shape))
    row4 = pl.BlockSpec((TOP_K, tr), lambda i: (0, i))
    return pl.pallas_call(
        functools.partial(_router_kernel, n_exp=n_exp, tr=tr),
        grid=(n // tr,),
        in_specs=[pl.BlockSpec((tr, d), lambda i: (i, 0)),
                  pl.BlockSpec((1, 6, d), lambda i: (i // per_b, 0, 0)),
                  full((1, d)), full((n_exp, d)), full((n_exp, d)), full((n_exp, 1)),
                  full((tr, tr + LANES))],
        out_specs=[pl.BlockSpec((tr * (d // LANES), LANES), lambda i: (i, 0)), row4, row4, row4,
                   full((n_exp, LANES))],
        out_shape=[jax.ShapeDtypeStruct((n * (d // LANES), LANES), F32),
                   jax.ShapeDtypeStruct((TOP_K, n), jnp.int32),
                   jax.ShapeDtypeStruct((TOP_K, n), F32),
                   jax.ShapeDtypeStruct((TOP_K, n), jnp.int32),
                   jax.ShapeDtypeStruct((n_exp, LANES), jnp.int32)],
        scratch_shapes=[pltpu.VMEM((n_exp, LANES), F32)],
        compiler_params=_params("arbitrary"),
        name="moe_router",
    )(x2, mod, g2, rwh, rwl, router_b[:, None], text)


def _zero_fill(fs_ref, fl_ref, nu_ref, zero_ref, xs_ref, zsem, *, eb, nck, n_exp, n_blocks, wait):
    def go(copy):
        if wait:
            copy.wait()
        else:
            copy.start()

    def rows(ref, first, count):
        return ref.at[pl.ds(pl.multiple_of(first * nck, nck), count * nck)]

    def per_expert(e, c):
        off = fs_ref[e]
        length = fl_ref[e]
        for bit in range(eb.bit_length() - 1):
            size = 1 << bit

            @pl.when((length & size) != 0)
            def _(off=off, size=size):
                go(pltpu.make_async_copy(rows(zero_ref, 0, size), rows(xs_ref, off, size), zsem))

            off = off + (length & size)
        return c

    lax.fori_loop(0, n_exp, per_expert, 0)

    def per_tail(j, c):
        blk = nu_ref[0] + j

        @pl.when(blk < n_blocks)
        def _():
            go(pltpu.make_async_copy(zero_ref, rows(xs_ref, blk * eb, eb), zsem))

        return c

    lax.fori_loop(0, n_exp, per_tail, 0)


def _dispatch_kernel(fs_ref, fl_ref, nu_ref, dest_ref, h_ref, xs_ref, zero_ref, sem, zsem,
                     *, td, eb, nck, n_exp, n_blocks):
    i = pl.program_id(0)
    fill = functools.partial(_zero_fill, fs_ref, fl_ref, nu_ref, zero_ref, xs_ref, zsem,
                             eb=eb, nck=nck, n_exp=n_exp, n_blocks=n_blocks)

    @pl.when(i == 0)
    def _():
        zero_ref[...] = jnp.zeros_like(zero_ref)
        fill(wait=False)

    def issue(t2, c):
        for u in range(ISSUE_UNROLL):
            t = t2 * ISSUE_UNROLL + u
            src = h_ref.at[pl.ds(pl.multiple_of(t * nck, nck), nck)]
            for k in range(TOP_K):
                slot = dest_ref[0, 0, t * TOP_K + k]
                dst = xs_ref.at[pl.ds(pl.multiple_of(slot * nck, nck), nck)]
                pltpu.make_async_copy(src, dst, sem).start(priority=k % 2)
        return c

    lax.fori_loop(0, td // ISSUE_UNROLL, issue, 0)
    for _ in range(TOP_K):
        pltpu.make_async_copy(h_ref, xs_ref.at[pl.ds(0, td * nck)], sem).wait()

    @pl.when(i == pl.num_programs(0) - 1)
    def _():
        fill(wait=True)


def _dispatch(h, dest_tok, fill_start, fill_len, n_used, n_blocks, eb, td):
    nck = h.shape[0] * TOP_K // dest_tok.shape[0]
    n = h.shape[0] // nck
    n_exp = fill_start.shape[0]
    grid_spec = pltpu.PrefetchScalarGridSpec(
        num_scalar_prefetch=3,
        grid=(n // td,),
        in_specs=[pl.BlockSpec((1, 1, td * TOP_K), lambda i, *_: (i, 0, 0), memory_space=pltpu.SMEM),
                  pl.BlockSpec((td * nck, LANES), lambda i, *_: (i, 0))],
        out_specs=pl.BlockSpec(memory_space=pl.ANY),
        scratch_shapes=[pltpu.VMEM((eb * nck, LANES), F32), pltpu.SemaphoreType.DMA,
                        pltpu.SemaphoreType.DMA],
    )
    return pl.pallas_call(
        functools.partial(_dispatch_kernel, td=td, eb=eb, nck=nck, n_exp=n_exp, n_blocks=n_blocks),
        grid_spec=grid_spec,
        out_shape=jax.ShapeDtypeStruct((n_blocks * eb * nck, LANES), F32),
        compiler_params=_params("arbitrary"),
        name="moe_dispatch",
    )(fill_start, fill_len, n_used, dest_tok.reshape(n // td, 1, td * TOP_K), h)


def _expert_kernel(be_ref, nu_ref, xs_ref, w1_ref, b1_ref, w2_ref, b2_ref, ys_ref, w1b_ref, w2b_ref,
                   *, eb, f, nck):
    i = pl.program_id(0)
    used = i < nu_ref[0]
    new_expert = jnp.logical_or(i == 0, be_ref[i] != be_ref[jnp.maximum(i - 1, 0)])

    @pl.when(jnp.logical_and(used, new_expert))
    def _():
        w1b_ref[...] = w1_ref[0, 0].astype(BF16)
        w2b_ref[...] = w2_ref[0, 0].astype(BF16)

    @pl.when(used)
    def _():
        x = jnp.concatenate([p.astype(BF16) for p in _load_rows(xs_ref, eb, nck)], axis=1)
        gu = _dot(x, w1b_ref[...]) + b1_ref[0, 0]
        g_h = jnp.minimum(gu[:, :f], SWIGLU_LIMIT)
        u_h = jnp.clip(gu[:, f:], -SWIGLU_LIMIT, SWIGLU_LIMIT)
        act = (u_h + 1.0) * (g_h * jax.nn.sigmoid(SWIGLU_ALPHA * g_h))
        _store_rows(ys_ref, _dot(act.astype(BF16), w2b_ref[...]) + b2_ref[0, 0], eb, nck)

    @pl.when(jnp.logical_not(used))
    def _():
        ys_ref[...] = jnp.zeros_like(ys_ref)


def _experts(xs, block_expert, n_used, layer, w1, b1, w2, b2, eb):
    depth, n_exp, d, f2 = w1.shape
    f = f2 // 2
    nck = d // LANES
    blk = (eb * nck, LANES)
    per_expert = lambda *tail: pl.BlockSpec((1, 1, *tail), lambda i, be, nu: (layer, be[i], 0, 0))
    grid_spec = pltpu.PrefetchScalarGridSpec(
        num_scalar_prefetch=2,
        grid=(xs.shape[0] // blk[0],),
        in_specs=[pl.BlockSpec(blk, lambda i, be, nu: (jnp.minimum(i, nu[0] - 1), 0)),
                  per_expert(d, f2), per_expert(1, f2), per_expert(f, d), per_expert(1, d)],
        out_specs=pl.BlockSpec(blk, lambda i, be, nu: (i, 0)),
        scratch_shapes=[pltpu.VMEM((d, f2), BF16), pltpu.VMEM((f, d), BF16)],
    )
    return pl.pallas_call(
        functools.partial(_expert_kernel, eb=eb, f=f, nck=nck),
        grid_spec=grid_spec,
        out_shape=jax.ShapeDtypeStruct(xs.shape, F32),
        compiler_params=_params("arbitrary"),
        name="moe_experts",
    )(block_expert, n_used, xs, w1, b1.reshape(depth, n_exp, 1, f2), w2, b2.reshape(depth, n_exp, 1, d))


def _combine_kernel(dest_ref, x_ref, gate_ref, mod_ref, ys_ref, o_ref, buf, sem, *, tc, nck):
    def issue(t2, c):
        for u in range(ISSUE_UNROLL):
            t = t2 * ISSUE_UNROLL + u
            for k in range(TOP_K):
                slot = dest_ref[0, 0, t * TOP_K + k]
                src = ys_ref.at[pl.ds(pl.multiple_of(slot * nck, nck), nck)]
                dst = buf.at[k, pl.ds(pl.multiple_of(t * nck, nck), nck)]
                pltpu.make_async_copy(src, dst, sem).start(priority=k % 2)
        return c

    lax.fori_loop(0, tc // ISSUE_UNROLL, issue, 0)
    for k in range(TOP_K):
        pltpu.make_async_copy(ys_ref.at[pl.ds(0, tc * nck)], buf.at[k], sem).wait()
    g = gate_ref[...]
    parts = [_load_rows(buf, tc, nck, lead=(k,)) for k in range(TOP_K)]
    for c in range(nck):
        lanes = slice(c * LANES, (c + 1) * LANES)
        moe = ((g[:, 0:1] * parts[0][c] + g[:, 1:2] * parts[1][c])
               + (g[:, 2:3] * parts[2][c] + g[:, 3:4] * parts[3][c]))
        o_ref[:, lanes] = x_ref[:, lanes] + mod_ref[0, 5:6, lanes] * moe


def _combine(x2, gates_tok, dest_tok, mod, ys, seq, tc):
    n, d = x2.shape
    nck = d // LANES
    per_b = seq // tc
    return pl.pallas_call(
        functools.partial(_combine_kernel, tc=tc, nck=nck),
        grid=(n // tc,),
        in_specs=[pl.BlockSpec((1, 1, tc * TOP_K), lambda i: (i, 0, 0), memory_space=pltpu.SMEM),
                  pl.BlockSpec((tc, d), lambda i: (i, 0)),
                  pl.BlockSpec((tc, TOP_K), lambda i: (i, 0)),
                  pl.BlockSpec((1, 6, d), lambda i: (i // per_b, 0, 0)),
                  pl.BlockSpec(memory_space=pl.ANY)],
        out_specs=pl.BlockSpec((tc, d), lambda i: (i, 0)),
        out_shape=jax.ShapeDtypeStruct((n, d), F32),
        scratch_shapes=[pltpu.VMEM((TOP_K, tc * nck, LANES), F32), pltpu.SemaphoreType.DMA],
        compiler_params=_params("arbitrary"),
        name="moe_combine",
    )(dest_tok.reshape(n // tc, 1, tc * TOP_K), x2, gates_tok, mod, ys)


def _moe(x, mod, g2, router_w, router_b, layer, w1, b1, w2, b2, tiles):
    b, s, d = x.shape
    n = b * s
    n_exp = router_w.shape[1]
    eb = tiles["eb"]
    x2 = x.reshape(n, d)
    h, idx, gates, rank, cnt = _router(x2, mod, g2, router_w, router_b, s, tiles["tr"])
    counts = cnt[:, 0]
    padded = (counts + eb - 1) // eb * eb
    pad_end = jnp.cumsum(padded)
    pad_start = pad_end - padded
    n_blocks = n * TOP_K // eb + n_exp
    is_e = idx[None] == jnp.arange(n_exp, dtype=jnp.int32)[:, None, None]
    dest = rank + jnp.sum(jnp.where(is_e, pad_start[:, None, None], 0), axis=0)
    dest_tok = dest.T.reshape(-1)
    block_row0 = jnp.arange(n_blocks, dtype=jnp.int32) * eb
    block_expert = jnp.minimum(jnp.sum(pad_end[None, :] <= block_row0[:, None], axis=1),
                               n_exp - 1).astype(jnp.int32)
    n_used = (pad_end[-1:] // eb).astype(jnp.int32)
    fill_start = pad_start + counts
    xs = _dispatch(h, dest_tok, fill_start, pad_end - fill_start, n_used, n_blocks, eb, tiles["td"])
    ys = _experts(xs, block_expert, n_used, layer, w1, b1, w2, b2, eb)
    out = _combine(x2, gates.T, dest_tok, mod, ys, s, tiles["tc"])
    return out.reshape(b, s, d)


def _tiles(seq):
    return {"ts": min(seq, 512), "tq": min(seq, 256), "tr": min(seq, 512),
            "td": min(seq, 1024), "tc": min(seq, 256), "eb": 512}


def kernel(x, c, norm1_g, norm2_g, ada_w, ada_b, conv_w_in, conv_w, conv_w_out, attn_w_qkv, attn_q_g, attn_k_g, attn_w_o, router_w, router_b, moe_w1, moe_b1, moe_w2, moe_b2):
    depth = ada_w.shape[0]
    t = _tiles(x.shape[1])
    mods = _ada(c, ada_w, ada_b)
    for i in range(depth):
        mod = mods[i]
        g1 = norm1_g[i][None, :]
        j = i // 2
        if i % 2 == 0:
            x = _conv_mixer(x, mod, g1, conv_w_in[j].astype(BF16), conv_w[j],
                            conv_w_out[j].astype(BF16), t["ts"])
        else:
            q, k, v = _qkv(x, mod, g1, attn_w_qkv[j].astype(BF16), attn_q_g[j], attn_k_g[j], t["ts"])
            o = _attention(q, k, v, t["tq"])
            x = _oproj(x, o, mod, attn_w_o[j].astype(BF16), t["ts"])
        x = _moe(x, mod, norm2_g[i][None, :], router_w[i], router_b[i],
                 i, moe_w1, moe_b1, moe_w2, moe_b2, t)
    return x
```

```python
import functools

import jax
import jax.numpy as jnp
from jax import lax
from jax.experimental import pallas as pl
from jax.experimental.pallas import tpu as pltpu

HEAD_DIM = 64
TOP_K = 4
CONV_W = 3
EPS = 1e-6
SWIGLU_LIMIT = 7.0
SWIGLU_ALPHA = 1.702
LANES = 128
SUBLANES = 8
HEADS_PER_TILE = LANES // HEAD_DIM
NEG_CUTOFF = -106.0
VMEM_LIMIT = 56 * 1024 * 1024
ISSUE_UNROLL = 4

F32 = jnp.float32
BF16 = jnp.bfloat16
_NT = (((1,), (1,)), ((), ()))


def _dot(a, b):
    return jnp.dot(a, b, preferred_element_type=F32)


def _split(a):
    hi = a.astype(BF16)
    lo = (a - hi.astype(F32)).astype(BF16)
    return hi, lo


def _modulate(x, g, sc, sh):
    ms = jnp.mean(x * x, axis=-1, keepdims=True)
    return (x * lax.rsqrt(ms + EPS)) * g * (1.0 + sc) + sh


def _load_rows(ref, n_rows, n_chunks, lead=()):
    return [ref[(*lead, pl.ds(c, n_rows, stride=n_chunks), slice(None))] for c in range(n_chunks)]


def _store_rows(ref, value, n_rows, n_chunks):
    for c in range(n_chunks):
        ref[pl.ds(c, n_rows, stride=n_chunks), :] = value[:, c * LANES:(c + 1) * LANES]


def _params(*sem):
    return pltpu.CompilerParams(dimension_semantics=sem, vmem_limit_bytes=VMEM_LIMIT)


def _ada_kernel(c_ref, w_ref, b_ref, o_ref):
    c = c_ref[...]
    ca = c * jax.nn.sigmoid(c)
    o_ref[0] = jnp.dot(ca, w_ref[0], preferred_element_type=F32,
                       precision=lax.Precision.HIGHEST) + b_ref[0]


def _ada(c, ada_w, ada_b):
    depth, d, d6 = ada_w.shape
    b = c.shape[0]
    tn = min(d6, 1536)
    out = pl.pallas_call(
        _ada_kernel,
        grid=(depth, d6 // tn),
        in_specs=[pl.BlockSpec((b, d), lambda i, j: (0, 0)),
                  pl.BlockSpec((1, d, tn), lambda i, j: (i, 0, j)),
                  pl.BlockSpec((1, 1, tn), lambda i, j: (i, 0, j))],
        out_specs=pl.BlockSpec((1, b, tn), lambda i, j: (i, 0, j)),
        out_shape=jax.ShapeDtypeStruct((depth, b, d6), F32),
        compiler_params=_params("parallel", "parallel"),
        name="ada",
    )(c, ada_w, ada_b.reshape(depth, 1, d6))
    return out.reshape(depth, b, 6, d)


def _conv_kernel(x_ref, mod_ref, g_ref, win_ref, cw_ref, wout_ref, o_ref, carry_ref, *, ts, d):
    s = pl.program_id(1)

    @pl.when(s == 0)
    def _():
        carry_ref[...] = jnp.zeros_like(carry_ref)

    x = x_ref[0]
    h = _modulate(x, g_ref[...], mod_ref[0, 1:2, :], mod_ref[0, 0:1, :]).astype(BF16)
    proj = _dot(h, win_ref[...])
    v = proj[:, d:2 * d] * proj[:, 2 * d:]
    prev = carry_ref[...]
    row = lax.broadcasted_iota(jnp.int32, (ts, 1), 0)
    v1 = jnp.where(row == 0, prev[7:8, :], pltpu.roll(v, 1, axis=0))
    v2 = jnp.where(row == 0, prev[6:7, :],
                   jnp.where(row == 1, prev[7:8, :], pltpu.roll(v, 2, axis=0)))
    y = cw_ref[2:3, :] * v + cw_ref[1:2, :] * v1 + cw_ref[0:1, :] * v2
    carry_ref[...] = v[ts - 8:, :]
    out = _dot((proj[:, :d] * y).astype(BF16), wout_ref[...])
    o_ref[0] = x + mod_ref[0, 2:3, :] * out


def _conv_mixer(x, mod, g1, w_in, conv_w, w_out, ts):
    b, s, d = x.shape
    return pl.pallas_call(
        functools.partial(_conv_kernel, ts=ts, d=d),
        grid=(b, s // ts),
        in_specs=[pl.BlockSpec((1, ts, d), lambda i, j: (i, j, 0)),
                  pl.BlockSpec((1, 6, d), lambda i, j: (i, 0, 0)),
                  pl.BlockSpec((1, d), lambda i, j: (0, 0)),
                  pl.BlockSpec((d, 3 * d), lambda i, j: (0, 0)),
                  pl.BlockSpec((CONV_W, d), lambda i, j: (0, 0)),
                  pl.BlockSpec((d, d), lambda i, j: (0, 0))],
        out_specs=pl.BlockSpec((1, ts, d), lambda i, j: (i, j, 0)),
        out_shape=jax.ShapeDtypeStruct((b, s, d), F32),
        scratch_shapes=[pltpu.VMEM((8, d), F32)],
        compiler_params=_params("parallel", "arbitrary"),
        name="conv_mixer",
    )(x, mod, g1, w_in, conv_w, w_out)


def _head_rms(t, gsum_ref, gexp_ref):
    ss = _dot((t * t).astype(BF16), gsum_ref[...])
    inv = lax.rsqrt(ss * (1.0 / HEAD_DIM) + EPS)
    ihi, ilo = _split(inv)
    return _dot(ihi, gexp_ref[...]) + _dot(ilo, gexp_ref[...])


def _qkv_kernel(x_ref, mod_ref, g_ref, w_ref, qg_ref, kg_ref, gsum_ref, gexp_ref,
                q_ref, k_ref, v_ref, *, d):
    x = x_ref[0]
    h = _modulate(x, g_ref[...], mod_ref[0, 1:2, :], mod_ref[0, 0:1, :]).astype(BF16)
    qkv = _dot(h, w_ref[...])
    q = qkv[:, :d]
    k = qkv[:, d:2 * d]
    q_ref[0] = (q * _head_rms(q, gsum_ref, gexp_ref) * (qg_ref[...] * HEAD_DIM ** -0.5)).astype(BF16)
    k_ref[0] = (k * _head_rms(k, gsum_ref, gexp_ref) * kg_ref[...]).astype(BF16)
    v_ref[0] = qkv[:, 2 * d:].astype(BF16)


def _qkv(x, mod, g1, w_qkv, q_g, k_g, ts):
    b, s, d = x.shape
    n_heads = d // HEAD_DIM
    head_of = jnp.arange(d) // HEAD_DIM
    gsum = (head_of[:, None] == jnp.arange(LANES)[None, :]).astype(BF16)
    gexp = gsum.T
    qg = jnp.tile(q_g, n_heads)[None, :]
    kg = jnp.tile(k_g, n_heads)[None, :]
    blk = pl.BlockSpec((1, ts, d), lambda i, j: (i, j, 0))
    full = lambda shape: pl.BlockSpec(shape, lambda i, j: (0,) * len(shape))
    return pl.pallas_call(
        functools.partial(_qkv_kernel, d=d),
        grid=(b, s // ts),
        in_specs=[blk, pl.BlockSpec((1, 6, d), lambda i, j: (i, 0, 0)), full((1, d)),
                  full((d, 3 * d)), full((1, d)), full((1, d)), full((d, LANES)), full((LANES, d))],
        out_specs=[blk, blk, blk],
        out_shape=[jax.ShapeDtypeStruct((b, s, d), BF16)] * 3,
        compiler_params=_params("parallel", "parallel"),
        name="qkv",
    )(x, mod, g1, w_qkv, qg, kg, gsum, gexp)


def _attn_kernel(q_ref, k_ref, v_ref, t_ref, o_ref, *, tq, nq):
    first_head = lax.broadcasted_iota(jnp.int32, (1, LANES), 1) < HEAD_DIM
    rows = HEADS_PER_TILE * tq
    row = lax.broadcasted_iota(jnp.int32, (rows, tq), 0)
    col = lax.broadcasted_iota(jnp.int32, (rows, tq), 1)
    causal = col < jnp.where(row >= tq, row - tq, row)

    def scores(q2, kb):
        start = pl.multiple_of(kb * tq, tq)
        z = lax.dot_general(q2, k_ref[0, pl.ds(start, tq), :], _NT, preferred_element_type=F32)
        return z, v_ref[0, pl.ds(start, tq), :]

    def logs(z, masked):
        log_beta = jnp.minimum(z, 0.0) - jnp.log(1.0 + jnp.exp(-jnp.abs(z)))
        l1m = log_beta - z
        return log_beta, (jnp.where(causal, l1m, 0.0) if masked else l1m)

    def row_total(suf, l1m):
        return suf[:, 0:1] + l1m[:, 0:1]

    q2s, qis, state = [], [], []
    for u in range(nq):
        qi = pl.program_id(2) * nq + u
        q = q_ref[0, u * tq:(u + 1) * tq, :]
        zero = jnp.zeros_like(q)
        q2 = jnp.concatenate([jnp.where(first_head, q, zero), jnp.where(first_head, zero, q)], axis=0)
        z_d, v_d = scores(q2, qi)
        z_p, v_p = scores(q2, jnp.maximum(qi - 1, 0))
        has_prev = (qi > 0).astype(F32)
        lb_d, lm_d = logs(z_d, True)
        lb_p, lm_p = logs(z_p, False)
        suf_d = _dot(lm_d.astype(BF16), t_ref[...])
        suf_p = _dot(lm_p.astype(BF16), t_ref[...])
        tot_d = row_total(suf_d, lm_d)
        a_d = jnp.where(causal, jnp.exp(lb_d + suf_d), 0.0)
        a_p = jnp.exp(lb_p + suf_p + tot_d)
        q2s.append(q2)
        qis.append(qi)
        state += [tot_d + has_prev * row_total(suf_p, lm_p),
                  _dot(a_d.astype(BF16), v_d) + has_prev * _dot(a_p.astype(BF16), v_p)]

    def cond(st):
        n = st[0]
        alive = [jnp.logical_and(qis[u] - 2 - n >= 0, jnp.max(st[1 + 2 * u]) > NEG_CUTOFF)
                 for u in range(nq)]
        return functools.reduce(jnp.logical_or, alive)

    def body(st):
        n = st[0]
        out = [n + 1]
        for u in range(nq):
            carry, acc = st[1 + 2 * u], st[2 + 2 * u]
            kb = qis[u] - 2 - n
            valid = (kb >= 0).astype(F32)
            z, vblk = scores(q2s[u], jnp.maximum(kb, 0))
            log_beta, l1m = logs(z, False)
            suf = _dot(l1m.astype(BF16), t_ref[...])
            a = jnp.exp(log_beta + suf + carry)
            out += [carry + valid * row_total(suf, l1m), acc + valid * _dot(a.astype(BF16), vblk)]
        return tuple(out)

    final = lax.while_loop(cond, body, (jnp.int32(0), *state))
    for u in range(nq):
        acc = final[2 + 2 * u]
        o_ref[0, u * tq:(u + 1) * tq, :] = jnp.where(first_head, acc[:tq], acc[tq:]).astype(o_ref.dtype)


def _attention(q, k, v, tq, nq):
    b, s, d = q.shape
    jj = jnp.arange(tq)
    text = (jj[:, None] > jj[None, :]).astype(BF16)
    return pl.pallas_call(
        functools.partial(_attn_kernel, tq=tq, nq=nq),
        grid=(b, d // LANES, s // (tq * nq)),
        in_specs=[pl.BlockSpec((1, tq * nq, LANES), lambda i, h, j: (i, j, h)),
                  pl.BlockSpec((1, s, LANES), lambda i, h, j: (i, 0, h)),
                  pl.BlockSpec((1, s, LANES), lambda i, h, j: (i, 0, h)),
                  pl.BlockSpec((tq, tq), lambda i, h, j: (0, 0))],
        out_specs=pl.BlockSpec((1, tq * nq, LANES), lambda i, h, j: (i, j, h)),
        out_shape=jax.ShapeDtypeStruct((b, s, d), BF16),
        compiler_params=_params("parallel", "parallel", "parallel"),
        name="stickbreak_attn",
    )(q, k, v, text)


def _oproj_kernel(x_ref, o_ref_in, mod_ref, w_ref, out_ref):
    out_ref[0] = x_ref[0] + mod_ref[0, 2:3, :] * _dot(o_ref_in[0], w_ref[...])


def _oproj(x, o, mod, w_o, ts):
    b, s, d = x.shape
    blk = pl.BlockSpec((1, ts, d), lambda i, j: (i, j, 0))
    return pl.pallas_call(
        _oproj_kernel,
        grid=(b, s // ts),
        in_specs=[blk, blk, pl.BlockSpec((1, 6, d), lambda i, j: (i, 0, 0)),
                  pl.BlockSpec((d, d), lambda i, j: (0, 0))],
        out_specs=blk,
        out_shape=jax.ShapeDtypeStruct((b, s, d), F32),
        compiler_params=_params("parallel", "parallel"),
        name="attn_oproj",
    )(x, o, mod, w_o)


def _router_kernel(x_ref, mod_ref, g_ref, rwh_ref, rwl_ref, rb_ref, tri_ref,
                   h_ref, idx_ref, gate_ref, rank_ref, cnt_ref, base_ref, *, n_exp, tr):
    @pl.when(pl.program_id(0) == 0)
    def _():
        base_ref[...] = jnp.zeros_like(base_ref)

    h = _modulate(x_ref[...], g_ref[...], mod_ref[0, 4:5, :], mod_ref[0, 3:4, :])
    _store_rows(h_ref, h, tr, h.shape[1] // LANES)
    hh, hl = _split(h)
    rwh = rwh_ref[...]
    logits = (lax.dot_general(rwh, hh, _NT, preferred_element_type=F32)
              + lax.dot_general(rwh, hl, _NT, preferred_element_type=F32)
              + lax.dot_general(rwl_ref[...], hh, _NT, preferred_element_type=F32)) + rb_ref[...]
    eio = lax.broadcasted_iota(jnp.int32, (n_exp, tr), 0)
    work = logits
    tops, onehots = [], []
    for k in range(TOP_K):
        m = jnp.max(work, axis=0, keepdims=True)
        am = jnp.min(jnp.where(work == m, eio, n_exp), axis=0, keepdims=True)
        oh = eio == am
        work = jnp.where(oh, -jnp.inf, work)
        tops.append(m)
        onehots.append(oh)
        idx_ref[k:k + 1, :] = am
    es = [jnp.exp(m - tops[0]) for m in tops]
    den = (es[0] + es[1]) + (es[2] + es[3])
    for k in range(TOP_K):
        gate_ref[k:k + 1, :] = es[k] / den
    sel = jnp.zeros((n_exp, tr), F32)
    for oh in onehots:
        sel = sel + oh.astype(F32)
    cum = _dot(sel.astype(BF16), tri_ref[...])
    base = base_ref[...]
    pos = cum[:, :tr] + jnp.concatenate([base] * (tr // LANES), axis=1)
    for k in range(TOP_K):
        rank_ref[k:k + 1, :] = jnp.sum(jnp.where(onehots[k], pos, 0.0), axis=0,
                                       keepdims=True).astype(jnp.int32)
    base = base + cum[:, tr:]
    base_ref[...] = base
    cnt_ref[...] = base.astype(jnp.int32)


def _router(x2, mod, g2, router_w, router_b, seq, tr):
    n, d = x2.shape
    n_exp = router_w.shape[1]
    rwh, rwl = _split(router_w.T)
    jj = jnp.arange(tr)
    tri = (jj[:, None] < jj[None, :]).astype(BF16)
    text = jnp.concatenate([tri, jnp.ones((tr, LANES), BF16)], axis=1)
    per_b = seq // tr
    full = lambda shape: pl.BlockSpec(shape, lambda i: (0,) * len(shape))
    row4 = pl.BlockSpec((TOP_K, tr), lambda i: (0, i))
    return pl.pallas_call(
        functools.partial(_router_kernel, n_exp=n_exp, tr=tr),
        grid=(n // tr,),
        in_specs=[pl.BlockSpec((tr, d), lambda i: (i, 0)),
                  pl.BlockSpec((1, 6, d), lambda i: (i // per_b, 0, 0)),
                  full((1, d)), full((n_exp, d)), full((n_exp, d)), full((n_exp, 1)),
                  full((tr, tr + LANES))],
        out_specs=[pl.BlockSpec((tr * (d // LANES), LANES), lambda i: (i, 0)), row4, row4, row4,
                   full((n_exp, LANES))],
        out_shape=[jax.ShapeDtypeStruct((n * (d // LANES), LANES), F32),
                   jax.ShapeDtypeStruct((TOP_K, n), jnp.int32),
                   jax.ShapeDtypeStruct((TOP_K, n), F32),
                   jax.ShapeDtypeStruct((TOP_K, n), jnp.int32),
                   jax.ShapeDtypeStruct((n_exp, LANES), jnp.int32)],
        scratch_shapes=[pltpu.VMEM((n_exp, LANES), F32)],
        compiler_params=_params("arbitrary"),
        name="moe_router",
    )(x2, mod, g2, rwh, rwl, router_b[:, None], text)


def _zero_fill(fs_ref, fl_ref, nu_ref, zero_ref, xs_ref, zsem, *, eb, nck, n_exp, n_blocks, wait):
    def go(copy):
        if wait:
            copy.wait()
        else:
            copy.start()

    def rows(ref, first, count):
        return ref.at[pl.ds(pl.multiple_of(first * nck, nck), count * nck)]

    def per_expert(e, c):
        off = fs_ref[e]
        length = fl_ref[e]
        for bit in range(eb.bit_length() - 1):
            size = 1 << bit

            @pl.when((length & size) != 0)
            def _(off=off, size=size):
                go(pltpu.make_async_copy(rows(zero_ref, 0, size), rows(xs_ref, off, size), zsem))

            off = off + (length & size)
        return c

    lax.fori_loop(0, n_exp, per_expert, 0)

    def per_tail(j, c):
        blk = nu_ref[0] + j

        @pl.when(blk < n_blocks)
        def _():
            go(pltpu.make_async_copy(zero_ref, rows(xs_ref, blk * eb, eb), zsem))

        return c

    lax.fori_loop(0, n_exp, per_tail, 0)


def _dispatch_kernel(fs_ref, fl_ref, nu_ref, dest_ref, h_ref, xs_ref, zero_ref, sem, zsem,
                     *, td, eb, nck, n_exp, n_blocks):
    i = pl.program_id(0)
    fill = functools.partial(_zero_fill, fs_ref, fl_ref, nu_ref, zero_ref, xs_ref, zsem,
                             eb=eb, nck=nck, n_exp=n_exp, n_blocks=n_blocks)

    @pl.when(i == 0)
    def _():
        zero_ref[...] = jnp.zeros_like(zero_ref)
        fill(wait=False)

    def issue(t2, c):
        for u in range(ISSUE_UNROLL):
            t = t2 * ISSUE_UNROLL + u
            src = h_ref.at[pl.ds(pl.multiple_of(t * nck, nck), nck)]
            for k in range(TOP_K):
                slot = dest_ref[0, 0, t * TOP_K + k]
                dst = xs_ref.at[pl.ds(pl.multiple_of(slot * nck, nck), nck)]
                pltpu.make_async_copy(src, dst, sem).start(priority=k % 2)
        return c

    lax.fori_loop(0, td // ISSUE_UNROLL, issue, 0)
    for _ in range(TOP_K):
        pltpu.make_async_copy(h_ref, xs_ref.at[pl.ds(0, td * nck)], sem).wait()

    @pl.when(i == pl.num_programs(0) - 1)
    def _():
        fill(wait=True)


def _dispatch(h, dest_tok, fill_start, fill_len, n_used, n_blocks, eb, td):
    nck = h.shape[0] * TOP_K // dest_tok.shape[0]
    n = h.shape[0] // nck
    n_exp = fill_start.shape[0]
    grid_spec = pltpu.PrefetchScalarGridSpec(
        num_scalar_prefetch=3,
        grid=(n // td,),
        in_specs=[pl.BlockSpec((1, 1, td * TOP_K), lambda i, *_: (i, 0, 0), memory_space=pltpu.SMEM),
                  pl.BlockSpec((td * nck, LANES), lambda i, *_: (i, 0))],
        out_specs=pl.BlockSpec(memory_space=pl.ANY),
        scratch_shapes=[pltpu.VMEM((eb * nck, LANES), F32), pltpu.SemaphoreType.DMA,
                        pltpu.SemaphoreType.DMA],
    )
    return pl.pallas_call(
        functools.partial(_dispatch_kernel, td=td, eb=eb, nck=nck, n_exp=n_exp, n_blocks=n_blocks),
        grid_spec=grid_spec,
        out_shape=jax.ShapeDtypeStruct((n_blocks * eb * nck, LANES), F32),
        compiler_params=_params("arbitrary"),
        name="moe_dispatch",
    )(fill_start, fill_len, n_used, dest_tok.reshape(n // td, 1, td * TOP_K), h)


def _expert_kernel(be_ref, nu_ref, xs_ref, w1_ref, b1_ref, w2_ref, b2_ref, ys_ref, w1b_ref, w2b_ref,
                   *, eb, f, nck):
    i = pl.program_id(0)
    used = i < nu_ref[0]
    new_expert = jnp.logical_or(i == 0, be_ref[i] != be_ref[jnp.maximum(i - 1, 0)])

    @pl.when(jnp.logical_and(used, new_expert))
    def _():
        w1b_ref[...] = w1_ref[0, 0].astype(BF16)
        w2b_ref[...] = w2_ref[0, 0].astype(BF16)

    @pl.when(used)
    def _():
        x = jnp.concatenate([p.astype(BF16) for p in _load_rows(xs_ref, eb, nck)], axis=1)
        gu = _dot(x, w1b_ref[...]) + b1_ref[0, 0]
        g_h = jnp.minimum(gu[:, :f], SWIGLU_LIMIT)
        u_h = jnp.clip(gu[:, f:], -SWIGLU_LIMIT, SWIGLU_LIMIT)
        act = (u_h + 1.0) * (g_h * jax.nn.sigmoid(SWIGLU_ALPHA * g_h))
        _store_rows(ys_ref, _dot(act.astype(BF16), w2b_ref[...]) + b2_ref[0, 0], eb, nck)

    @pl.when(jnp.logical_not(used))
    def _():
        ys_ref[...] = jnp.zeros_like(ys_ref)


def _experts(xs, block_expert, n_used, layer, w1, b1, w2, b2, eb):
    depth, n_exp, d, f2 = w1.shape
    f = f2 // 2
    nck = d // LANES
    blk = (eb * nck, LANES)
    per_expert = lambda *tail: pl.BlockSpec((1, 1, *tail), lambda i, be, nu: (layer, be[i], 0, 0))
    grid_spec = pltpu.PrefetchScalarGridSpec(
        num_scalar_prefetch=2,
        grid=(xs.shape[0] // blk[0],),
        in_specs=[pl.BlockSpec(blk, lambda i, be, nu: (jnp.minimum(i, nu[0] - 1), 0)),
                  per_expert(d, f2), per_expert(1, f2), per_expert(f, d), per_expert(1, d)],
        out_specs=pl.BlockSpec(blk, lambda i, be, nu: (i, 0)),
        scratch_shapes=[pltpu.VMEM((d, f2), BF16), pltpu.VMEM((f, d), BF16)],
    )
    return pl.pallas_call(
        functools.partial(_expert_kernel, eb=eb, f=f, nck=nck),
        grid_spec=grid_spec,
        out_shape=jax.ShapeDtypeStruct(xs.shape, F32),
        compiler_params=_params("arbitrary"),
        name="moe_experts",
    )(block_expert, n_used, xs, w1, b1.reshape(depth, n_exp, 1, f2), w2, b2.reshape(depth, n_exp, 1, d))


def _combine_kernel(dest_ref, dest_next_ref, x_ref, gate_ref, mod_ref, ys_ref, o_ref, buf, sems,
                    *, tc, nck):
    i = pl.program_id(0)
    cur = i % 2

    def gather(slots_ref, half):
        def issue(t2, c):
            for u in range(ISSUE_UNROLL):
                t = t2 * ISSUE_UNROLL + u
                for k in range(TOP_K):
                    slot = slots_ref[0, 0, t * TOP_K + k]
                    src = ys_ref.at[pl.ds(pl.multiple_of(slot * nck, nck), nck)]
                    dst = buf.at[half, k, pl.ds(pl.multiple_of(t * nck, nck), nck)]
                    pltpu.make_async_copy(src, dst, sems.at[half]).start(priority=k % 2)
            return c

        lax.fori_loop(0, tc // ISSUE_UNROLL, issue, 0)

    @pl.when(i == 0)
    def _():
        gather(dest_ref, 0)

    @pl.when(i + 1 < pl.num_programs(0))
    def _():
        gather(dest_next_ref, 1 - cur)

    for k in range(TOP_K):
        pltpu.make_async_copy(ys_ref.at[pl.ds(0, tc * nck)], buf.at[cur, k], sems.at[cur]).wait()
    g = gate_ref[...]
    parts = [_load_rows(buf, tc, nck, lead=(cur, k)) for k in range(TOP_K)]
    for c in range(nck):
        lanes = slice(c * LANES, (c + 1) * LANES)
        moe = ((g[:, 0:1] * parts[0][c] + g[:, 1:2] * parts[1][c])
               + (g[:, 2:3] * parts[2][c] + g[:, 3:4] * parts[3][c]))
        o_ref[:, lanes] = x_ref[:, lanes] + mod_ref[0, 5:6, lanes] * moe


def _combine(x2, gates_tok, dest_tok, mod, ys, seq, tc):
    n, d = x2.shape
    nck = d // LANES
    per_b = seq // tc
    n_tiles = n // tc
    slots = dest_tok.reshape(n_tiles, 1, tc * TOP_K)
    return pl.pallas_call(
        functools.partial(_combine_kernel, tc=tc, nck=nck),
        grid=(n_tiles,),
        in_specs=[pl.BlockSpec((1, 1, tc * TOP_K), lambda i: (i, 0, 0), memory_space=pltpu.SMEM),
                  pl.BlockSpec((1, 1, tc * TOP_K), lambda i: (jnp.minimum(i + 1, n_tiles - 1), 0, 0),
                               memory_space=pltpu.SMEM),
                  pl.BlockSpec((tc, d), lambda i: (i, 0)),
                  pl.BlockSpec((tc, TOP_K), lambda i: (i, 0)),
                  pl.BlockSpec((1, 6, d), lambda i: (i // per_b, 0, 0)),
                  pl.BlockSpec(memory_space=pl.ANY)],
        out_specs=pl.BlockSpec((tc, d), lambda i: (i, 0)),
        out_shape=jax.ShapeDtypeStruct((n, d), F32),
        scratch_shapes=[pltpu.VMEM((2, TOP_K, tc * nck, LANES), F32), pltpu.SemaphoreType.DMA((2,))],
        compiler_params=_params("arbitrary"),
        name="moe_combine",
    )(slots, slots, x2, gates_tok, mod, ys)


def _moe(x, mod, g2, router_w, router_b, layer, w1, b1, w2, b2, tiles):
    b, s, d = x.shape
    n = b * s
    n_exp = router_w.shape[1]
    eb = tiles["eb"]
    x2 = x.reshape(n, d)
    h, idx, gates, rank, cnt = _router(x2, mod, g2, router_w, router_b, s, tiles["tr"])
    counts = cnt[:, 0]
    padded = (counts + eb - 1) // eb * eb
    pad_end = jnp.cumsum(padded)
    pad_start = pad_end - padded
    n_blocks = n * TOP_K // eb + n_exp
    is_e = idx[None] == jnp.arange(n_exp, dtype=jnp.int32)[:, None, None]
    dest = rank + jnp.sum(jnp.where(is_e, pad_start[:, None, None], 0), axis=0)
    dest_tok = dest.T.reshape(-1)
    block_row0 = jnp.arange(n_blocks, dtype=jnp.int32) * eb
    block_expert = jnp.minimum(jnp.sum(pad_end[None, :] <= block_row0[:, None], axis=1),
                               n_exp - 1).astype(jnp.int32)
    n_used = (pad_end[-1:] // eb).astype(jnp.int32)
    fill_start = pad_start + counts
    xs = _dispatch(h, dest_tok, fill_start, pad_end - fill_start, n_used, n_blocks, eb, tiles["td"])
    ys = _experts(xs, block_expert, n_used, layer, w1, b1, w2, b2, eb)
    out = _combine(x2, gates.T, dest_tok, mod, ys, s, tiles["tc"])
    return out.reshape(b, s, d)


def _tiles(seq):
    tq = min(seq, 256)
    return {"ts": min(seq, 512), "tq": tq, "nq": min(seq // tq, 4), "tr": min(seq, 512),
            "td": min(seq, 1024), "tc": min(seq, 256), "eb": 512}


def kernel(x, c, norm1_g, norm2_g, ada_w, ada_b, conv_w_in, conv_w, conv_w_out, attn_w_qkv, attn_q_g, attn_k_g, attn_w_o, router_w, router_b, moe_w1, moe_b1, moe_w2, moe_b2):
    depth = ada_w.shape[0]
    t = _tiles(x.shape[1])
    mods = _ada(c, ada_w, ada_b)
    for i in range(depth):
        mod = mods[i]
        g1 = norm1_g[i][None, :]
        j = i // 2
        if i % 2 == 0:
            x = _conv_mixer(x, mod, g1, conv_w_in[j].astype(BF16), conv_w[j],
                            conv_w_out[j].astype(BF16), t["ts"])
        else:
            q, k, v = _qkv(x, mod, g1, attn_w_qkv[j].astype(BF16), attn_q_g[j], attn_k_g[j], t["ts"])
            o = _attention(q, k, v, t["tq"], t["nq"])
            x = _oproj(x, o, mod, attn_w_o[j].astype(BF16), t["ts"])
        x = _moe(x, mod, norm2_g[i][None, :], router_w[i], router_b[i],
                 i, moe_w1, moe_b1, moe_w2, moe_b2, t)
    return x
```

```python
import functools

import jax
import jax.numpy as jnp
from jax import lax
from jax.experimental import pallas as pl
from jax.experimental.pallas import tpu as pltpu

HEAD_DIM = 64
TOP_K = 4
CONV_W = 3
EPS = 1e-6
SWIGLU_LIMIT = 7.0
SWIGLU_ALPHA = 1.702
LANES = 128
SUBLANES = 8
HEADS_PER_TILE = LANES // HEAD_DIM
NEG_CUTOFF = -106.0
VMEM_LIMIT = 56 * 1024 * 1024
ISSUE_UNROLL = 4
COMBINE_GROUP = 16

F32 = jnp.float32
BF16 = jnp.bfloat16
_NT = (((1,), (1,)), ((), ()))


def _dot(a, b):
    return jnp.dot(a, b, preferred_element_type=F32)


def _split(a):
    hi = a.astype(BF16)
    lo = (a - hi.astype(F32)).astype(BF16)
    return hi, lo


def _modulate(x, g, sc, sh):
    ms = jnp.mean(x * x, axis=-1, keepdims=True)
    return (x * lax.rsqrt(ms + EPS)) * g * (1.0 + sc) + sh


def _load_rows(ref, n_rows, n_chunks, lead=(), first=0):
    return [ref[(*lead, pl.ds(first * n_chunks + c, n_rows, stride=n_chunks), slice(None))]
            for c in range(n_chunks)]


def _store_rows(ref, value, n_rows, n_chunks, first=0):
    for c in range(n_chunks):
        ref[pl.ds(first * n_chunks + c, n_rows, stride=n_chunks), :] = value[:, c * LANES:(c + 1) * LANES]


def _params(*sem):
    return pltpu.CompilerParams(dimension_semantics=sem, vmem_limit_bytes=VMEM_LIMIT)


def _ada_kernel(c_ref, w_ref, b_ref, o_ref):
    c = c_ref[...]
    ca = c * jax.nn.sigmoid(c)
    o_ref[0] = jnp.dot(ca, w_ref[0], preferred_element_type=F32,
                       precision=lax.Precision.HIGHEST) + b_ref[0]


def _ada(c, ada_w, ada_b):
    depth, d, d6 = ada_w.shape
    b = c.shape[0]
    tn = min(d6, 1536)
    out = pl.pallas_call(
        _ada_kernel,
        grid=(depth, d6 // tn),
        in_specs=[pl.BlockSpec((b, d), lambda i, j: (0, 0)),
                  pl.BlockSpec((1, d, tn), lambda i, j: (i, 0, j)),
                  pl.BlockSpec((1, 1, tn), lambda i, j: (i, 0, j))],
        out_specs=pl.BlockSpec((1, b, tn), lambda i, j: (i, 0, j)),
        out_shape=jax.ShapeDtypeStruct((depth, b, d6), F32),
        compiler_params=_params("parallel", "parallel"),
        name="ada",
    )(c, ada_w, ada_b.reshape(depth, 1, d6))
    return out.reshape(depth, b, 6, d)


def _route_tile(x, first, mod_ref, g_ref, rwh_ref, rwl_ref, rb_ref, tri_ref,
                h_ref, idx_ref, gate_ref, rank_ref, cnt_ref, base_ref):
    n_exp = rwh_ref.shape[0]
    tr = x.shape[0]

    @pl.when(first)
    def _():
        base_ref[...] = jnp.zeros_like(base_ref)

    h = _modulate(x, g_ref[...], mod_ref[0, 4:5, :], mod_ref[0, 3:4, :])
    _store_rows(h_ref, h, tr, h.shape[1] // LANES)
    hh, hl = _split(h)
    rwh = rwh_ref[...]
    logits = (lax.dot_general(rwh, hh, _NT, preferred_element_type=F32)
              + lax.dot_general(rwh, hl, _NT, preferred_element_type=F32)
              + lax.dot_general(rwl_ref[...], hh, _NT, preferred_element_type=F32)) + rb_ref[...]
    eio = lax.broadcasted_iota(jnp.int32, (n_exp, tr), 0)
    work = logits
    tops, onehots = [], []
    for k in range(TOP_K):
        m = jnp.max(work, axis=0, keepdims=True)
        am = jnp.min(jnp.where(work == m, eio, n_exp), axis=0, keepdims=True)
        oh = eio == am
        work = jnp.where(oh, -jnp.inf, work)
        tops.append(m)
        onehots.append(oh)
        idx_ref[k:k + 1, :] = am
    es = [jnp.exp(m - tops[0]) for m in tops]
    den = (es[0] + es[1]) + (es[2] + es[3])
    for k in range(TOP_K):
        gate_ref[k:k + 1, :] = es[k] / den
    sel = jnp.zeros((n_exp, tr), F32)
    for oh in onehots:
        sel = sel + oh.astype(F32)
    cum = _dot(sel.astype(BF16), tri_ref[...])
    base = base_ref[...]
    pos = cum[:, :tr] + jnp.concatenate([base] * (tr // LANES), axis=1)
    for k in range(TOP_K):
        rank_ref[k:k + 1, :] = jnp.sum(jnp.where(onehots[k], pos, 0.0), axis=0,
                                       keepdims=True).astype(jnp.int32)
    base = base + cum[:, tr:]
    base_ref[...] = base
    cnt_ref[...] = base.astype(jnp.int32)


N_ROUTE_IN, N_ROUTE_OUT = 5, 5


def _route_operands(g2, router_w, router_b, n, d, tr, tiles_per_batch):
    n_exp = router_w.shape[1]
    rwh, rwl = _split(router_w.T)
    jj = jnp.arange(tr)
    tri = (jj[:, None] < jj[None, :]).astype(BF16)
    text = jnp.concatenate([tri, jnp.ones((tr, LANES), BF16)], axis=1)
    full = lambda shape: pl.BlockSpec(shape, lambda i, j: (0,) * len(shape))
    tile = lambda i, j: i * tiles_per_batch + j
    row4 = pl.BlockSpec((TOP_K, tr), lambda i, j: (0, tile(i, j)))
    arrays = [g2, rwh, rwl, router_b[:, None], text]
    in_specs = [full((1, d)), full((n_exp, d)), full((n_exp, d)), full((n_exp, 1)),
                full((tr, tr + LANES))]
    out_specs = [pl.BlockSpec((tr * (d // LANES), LANES), lambda i, j: (tile(i, j), 0)),
                 row4, row4, row4, full((n_exp, LANES))]
    out_shape = [jax.ShapeDtypeStruct((n * (d // LANES), LANES), F32),
                 jax.ShapeDtypeStruct((TOP_K, n), jnp.int32),
                 jax.ShapeDtypeStruct((TOP_K, n), F32),
                 jax.ShapeDtypeStruct((TOP_K, n), jnp.int32),
                 jax.ShapeDtypeStruct((n_exp, LANES), jnp.int32)]
    scratch = [pltpu.VMEM((n_exp, LANES), F32)]
    return arrays, in_specs, out_specs, out_shape, scratch


def _conv_kernel(x_ref, mod_ref, g_ref, win_ref, cw_ref, wout_ref, *rest, ts, d):
    route_in, (o_ref, *route_out), (carry_ref, base_ref) = (
        rest[:N_ROUTE_IN], rest[N_ROUTE_IN:N_ROUTE_IN + 1 + N_ROUTE_OUT], rest[N_ROUTE_IN + 1 + N_ROUTE_OUT:])
    b, s = pl.program_id(0), pl.program_id(1)

    @pl.when(s == 0)
    def _():
        carry_ref[...] = jnp.zeros_like(carry_ref)

    x = x_ref[0]
    h = _modulate(x, g_ref[...], mod_ref[0, 1:2, :], mod_ref[0, 0:1, :]).astype(BF16)
    proj = _dot(h, win_ref[...])
    v = proj[:, d:2 * d] * proj[:, 2 * d:]
    prev = carry_ref[...]
    row = lax.broadcasted_iota(jnp.int32, (ts, 1), 0)
    v1 = jnp.where(row == 0, prev[7:8, :], pltpu.roll(v, 1, axis=0))
    v2 = jnp.where(row == 0, prev[6:7, :],
                   jnp.where(row == 1, prev[7:8, :], pltpu.roll(v, 2, axis=0)))
    y = cw_ref[2:3, :] * v + cw_ref[1:2, :] * v1 + cw_ref[0:1, :] * v2
    carry_ref[...] = v[ts - 8:, :]
    out = _dot((proj[:, :d] * y).astype(BF16), wout_ref[...])
    x_new = x + mod_ref[0, 2:3, :] * out
    o_ref[0] = x_new
    _route_tile(x_new, jnp.logical_and(b == 0, s == 0), mod_ref, *route_in, *route_out, base_ref)


def _conv_mixer(x, mod, g1, w_in, conv_w, w_out, route, ts):
    b, s, d = x.shape
    r_arrays, r_in, r_out, r_shape, r_scratch = _route_operands(*route, b * s, d, ts, s // ts)
    return pl.pallas_call(
        functools.partial(_conv_kernel, ts=ts, d=d),
        grid=(b, s // ts),
        in_specs=[pl.BlockSpec((1, ts, d), lambda i, j: (i, j, 0)),
                  pl.BlockSpec((1, 6, d), lambda i, j: (i, 0, 0)),
                  pl.BlockSpec((1, d), lambda i, j: (0, 0)),
                  pl.BlockSpec((d, 3 * d), lambda i, j: (0, 0)),
                  pl.BlockSpec((CONV_W, d), lambda i, j: (0, 0)),
                  pl.BlockSpec((d, d), lambda i, j: (0, 0)), *r_in],
        out_specs=[pl.BlockSpec((1, ts, d), lambda i, j: (i, j, 0)), *r_out],
        out_shape=[jax.ShapeDtypeStruct((b, s, d), F32), *r_shape],
        scratch_shapes=[pltpu.VMEM((8, d), F32), *r_scratch],
        compiler_params=_params("arbitrary", "arbitrary"),
        name="conv_mixer",
    )(x, mod, g1, w_in, conv_w, w_out, *r_arrays)


def _head_rms(t, gsum_ref, gexp_ref):
    ss = _dot((t * t).astype(BF16), gsum_ref[...])
    inv = lax.rsqrt(ss * (1.0 / HEAD_DIM) + EPS)
    ihi, ilo = _split(inv)
    return _dot(ihi, gexp_ref[...]) + _dot(ilo, gexp_ref[...])


def _qkv_kernel(x_ref, mod_ref, g_ref, w_ref, qg_ref, kg_ref, gsum_ref, gexp_ref,
                q_ref, k_ref, v_ref, *, d):
    x = x_ref[0]
    h = _modulate(x, g_ref[...], mod_ref[0, 1:2, :], mod_ref[0, 0:1, :]).astype(BF16)
    qkv = _dot(h, w_ref[...])
    q = qkv[:, :d]
    k = qkv[:, d:2 * d]
    q_ref[0] = (q * _head_rms(q, gsum_ref, gexp_ref) * (qg_ref[...] * HEAD_DIM ** -0.5)).astype(BF16)
    k_ref[0] = (k * _head_rms(k, gsum_ref, gexp_ref) * kg_ref[...]).astype(BF16)
    v_ref[0] = qkv[:, 2 * d:].astype(BF16)


def _qkv(x, mod, g1, w_qkv, q_g, k_g, ts):
    b, s, d = x.shape
    n_heads = d // HEAD_DIM
    head_of = jnp.arange(d) // HEAD_DIM
    gsum = (head_of[:, None] == jnp.arange(LANES)[None, :]).astype(BF16)
    gexp = gsum.T
    qg = jnp.tile(q_g, n_heads)[None, :]
    kg = jnp.tile(k_g, n_heads)[None, :]
    blk = pl.BlockSpec((1, ts, d), lambda i, j: (i, j, 0))
    full = lambda shape: pl.BlockSpec(shape, lambda i, j: (0,) * len(shape))
    return pl.pallas_call(
        functools.partial(_qkv_kernel, d=d),
        grid=(b, s // ts),
        in_specs=[blk, pl.BlockSpec((1, 6, d), lambda i, j: (i, 0, 0)), full((1, d)),
                  full((d, 3 * d)), full((1, d)), full((1, d)), full((d, LANES)), full((LANES, d))],
        out_specs=[blk, blk, blk],
        out_shape=[jax.ShapeDtypeStruct((b, s, d), BF16)] * 3,
        compiler_params=_params("parallel", "parallel"),
        name="qkv",
    )(x, mod, g1, w_qkv, qg, kg, gsum, gexp)


def _attn_kernel(q_ref, k_ref, v_ref, t_ref, o_ref, *, tq, nq):
    first_head = lax.broadcasted_iota(jnp.int32, (1, LANES), 1) < HEAD_DIM
    rows = HEADS_PER_TILE * tq
    row = lax.broadcasted_iota(jnp.int32, (rows, tq), 0)
    col = lax.broadcasted_iota(jnp.int32, (rows, tq), 1)
    causal = col < jnp.where(row >= tq, row - tq, row)

    def scores(q2, kb):
        start = pl.multiple_of(kb * tq, tq)
        z = lax.dot_general(q2, k_ref[0, pl.ds(start, tq), :], _NT, preferred_element_type=F32)
        return z, v_ref[0, pl.ds(start, tq), :]

    def logs(z, masked):
        log_beta = jnp.minimum(z, 0.0) - jnp.log(1.0 + jnp.exp(-jnp.abs(z)))
        l1m = log_beta - z
        return log_beta, (jnp.where(causal, l1m, 0.0) if masked else l1m)

    def row_total(suf, l1m):
        return suf[:, 0:1] + l1m[:, 0:1]

    q2s, qis, state = [], [], []
    for u in range(nq):
        qi = pl.program_id(2) * nq + u
        q = q_ref[0, u * tq:(u + 1) * tq, :]
        zero = jnp.zeros_like(q)
        q2 = jnp.concatenate([jnp.where(first_head, q, zero), jnp.where(first_head, zero, q)], axis=0)
        z_d, v_d = scores(q2, qi)
        z_p, v_p = scores(q2, jnp.maximum(qi - 1, 0))
        has_prev = (qi > 0).astype(F32)
        lb_d, lm_d = logs(z_d, True)
        lb_p, lm_p = logs(z_p, False)
        suf_d = _dot(lm_d.astype(BF16), t_ref[...])
        suf_p = _dot(lm_p.astype(BF16), t_ref[...])
        tot_d = row_total(suf_d, lm_d)
        a_d = jnp.where(causal, jnp.exp(lb_d + suf_d), 0.0)
        a_p = jnp.exp(lb_p + suf_p + tot_d)
        q2s.append(q2)
        qis.append(qi)
        state += [tot_d + has_prev * row_total(suf_p, lm_p),
                  _dot(a_d.astype(BF16), v_d) + has_prev * _dot(a_p.astype(BF16), v_p)]

    def cond(st):
        n = st[0]
        alive = [jnp.logical_and(qis[u] - 2 - n >= 0, jnp.max(st[1 + 2 * u]) > NEG_CUTOFF)
                 for u in range(nq)]
        return functools.reduce(jnp.logical_or, alive)

    def body(st):
        n = st[0]
        out = [n + 1]
        for u in range(nq):
            carry, acc = st[1 + 2 * u], st[2 + 2 * u]
            kb = qis[u] - 2 - n
            valid = (kb >= 0).astype(F32)
            z, vblk = scores(q2s[u], jnp.maximum(kb, 0))
            log_beta, l1m = logs(z, False)
            suf = _dot(l1m.astype(BF16), t_ref[...])
            a = jnp.exp(log_beta + suf + carry)
            out += [carry + valid * row_total(suf, l1m), acc + valid * _dot(a.astype(BF16), vblk)]
        return tuple(out)

    final = lax.while_loop(cond, body, (jnp.int32(0), *state))
    for u in range(nq):
        acc = final[2 + 2 * u]
        o_ref[0, u * tq:(u + 1) * tq, :] = jnp.where(first_head, acc[:tq], acc[tq:]).astype(o_ref.dtype)


def _attention(q, k, v, tq, nq):
    b, s, d = q.shape
    jj = jnp.arange(tq)
    text = (jj[:, None] > jj[None, :]).astype(BF16)
    return pl.pallas_call(
        functools.partial(_attn_kernel, tq=tq, nq=nq),
        grid=(b, d // LANES, s // (tq * nq)),
        in_specs=[pl.BlockSpec((1, tq * nq, LANES), lambda i, h, j: (i, j, h)),
                  pl.BlockSpec((1, s, LANES), lambda i, h, j: (i, 0, h)),
                  pl.BlockSpec((1, s, LANES), lambda i, h, j: (i, 0, h)),
                  pl.BlockSpec((tq, tq), lambda i, h, j: (0, 0))],
        out_specs=pl.BlockSpec((1, tq * nq, LANES), lambda i, h, j: (i, j, h)),
        out_shape=jax.ShapeDtypeStruct((b, s, d), BF16),
        compiler_params=_params("parallel", "parallel", "parallel"),
        name="stickbreak_attn",
    )(q, k, v, text)


def _oproj_kernel(x_ref, o_ref_in, mod_ref, w_ref, *rest):
    route_in, (out_ref, *route_out), (base_ref,) = (
        rest[:N_ROUTE_IN], rest[N_ROUTE_IN:N_ROUTE_IN + 1 + N_ROUTE_OUT], rest[N_ROUTE_IN + 1 + N_ROUTE_OUT:])
    x_new = x_ref[0] + mod_ref[0, 2:3, :] * _dot(o_ref_in[0], w_ref[...])
    out_ref[0] = x_new
    first = jnp.logical_and(pl.program_id(0) == 0, pl.program_id(1) == 0)
    _route_tile(x_new, first, mod_ref, *route_in, *route_out, base_ref)


def _oproj(x, o, mod, w_o, route, ts):
    b, s, d = x.shape
    blk = pl.BlockSpec((1, ts, d), lambda i, j: (i, j, 0))
    r_arrays, r_in, r_out, r_shape, r_scratch = _route_operands(*route, b * s, d, ts, s // ts)
    return pl.pallas_call(
        _oproj_kernel,
        grid=(b, s // ts),
        in_specs=[blk, blk, pl.BlockSpec((1, 6, d), lambda i, j: (i, 0, 0)),
                  pl.BlockSpec((d, d), lambda i, j: (0, 0)), *r_in],
        out_specs=[blk, *r_out],
        out_shape=[jax.ShapeDtypeStruct((b, s, d), F32), *r_shape],
        scratch_shapes=r_scratch,
        compiler_params=_params("arbitrary", "arbitrary"),
        name="attn_oproj",
    )(x, o, mod, w_o, *r_arrays)


def _zero_fill(fs_ref, fl_ref, nu_ref, zero_ref, xs_ref, zsem, *, eb, nck, n_exp, n_blocks, wait):
    def go(copy):
        if wait:
            copy.wait()
        else:
            copy.start()

    def rows(ref, first, count):
        return ref.at[pl.ds(pl.multiple_of(first * nck, nck), count * nck)]

    def per_expert(e, c):
        off = fs_ref[e]
        length = fl_ref[e]
        for bit in range(eb.bit_length() - 1):
            size = 1 << bit

            @pl.when((length & size) != 0)
            def _(off=off, size=size):
                go(pltpu.make_async_copy(rows(zero_ref, 0, size), rows(xs_ref, off, size), zsem))

            off = off + (length & size)
        return c

    lax.fori_loop(0, n_exp, per_expert, 0)

    def per_tail(j, c):
        blk = nu_ref[0] + j

        @pl.when(blk < n_blocks)
        def _():
            go(pltpu.make_async_copy(zero_ref, rows(xs_ref, blk * eb, eb), zsem))

        return c

    lax.fori_loop(0, n_exp, per_tail, 0)


def _dispatch_kernel(fs_ref, fl_ref, nu_ref, dest_ref, h_ref, xs_ref, zero_ref, sem, zsem,
                     *, td, eb, nck, n_exp, n_blocks):
    i = pl.program_id(0)
    fill = functools.partial(_zero_fill, fs_ref, fl_ref, nu_ref, zero_ref, xs_ref, zsem,
                             eb=eb, nck=nck, n_exp=n_exp, n_blocks=n_blocks)

    @pl.when(i == 0)
    def _():
        zero_ref[...] = jnp.zeros_like(zero_ref)
        fill(wait=False)

    def issue(t2, c):
        for u in range(ISSUE_UNROLL):
            t = t2 * ISSUE_UNROLL + u
            src = h_ref.at[pl.ds(pl.multiple_of(t * nck, nck), nck)]
            for k in range(TOP_K):
                slot = dest_ref[0, 0, t * TOP_K + k]
                dst = xs_ref.at[pl.ds(pl.multiple_of(slot * nck, nck), nck)]
                pltpu.make_async_copy(src, dst, sem).start(priority=k % 2)
        return c

    lax.fori_loop(0, td // ISSUE_UNROLL, issue, 0)
    for _ in range(TOP_K):
        pltpu.make_async_copy(h_ref, xs_ref.at[pl.ds(0, td * nck)], sem).wait()

    @pl.when(i == pl.num_programs(0) - 1)
    def _():
        fill(wait=True)


def _dispatch(h, dest_tok, fill_start, fill_len, n_used, n_blocks, eb, td):
    nck = h.shape[0] * TOP_K // dest_tok.shape[0]
    n = h.shape[0] // nck
    n_exp = fill_start.shape[0]
    grid_spec = pltpu.PrefetchScalarGridSpec(
        num_scalar_prefetch=3,
        grid=(n // td,),
        in_specs=[pl.BlockSpec((1, 1, td * TOP_K), lambda i, *_: (i, 0, 0), memory_space=pltpu.SMEM),
                  pl.BlockSpec((td * nck, LANES), lambda i, *_: (i, 0))],
        out_specs=pl.BlockSpec(memory_space=pl.ANY),
        scratch_shapes=[pltpu.VMEM((eb * nck, LANES), F32), pltpu.SemaphoreType.DMA,
                        pltpu.SemaphoreType.DMA],
    )
    return pl.pallas_call(
        functools.partial(_dispatch_kernel, td=td, eb=eb, nck=nck, n_exp=n_exp, n_blocks=n_blocks),
        grid_spec=grid_spec,
        out_shape=jax.ShapeDtypeStruct((n_blocks * eb * nck, LANES), F32),
        compiler_params=_params("arbitrary"),
        name="moe_dispatch",
    )(fill_start, fill_len, n_used, dest_tok.reshape(n // td, 1, td * TOP_K), h)


def _expert_kernel(be_ref, nu_ref, xs_ref, w1_ref, b1_ref, w2_ref, b2_ref, ys_ref, w1b_ref, w2b_ref,
                   *, eb, f, nck):
    i = pl.program_id(0)
    used = i < nu_ref[0]
    new_expert = jnp.logical_or(i == 0, be_ref[i] != be_ref[jnp.maximum(i - 1, 0)])

    @pl.when(jnp.logical_and(used, new_expert))
    def _():
        w1b_ref[...] = w1_ref[0, 0].astype(BF16)
        w2b_ref[...] = w2_ref[0, 0].astype(BF16)

    @pl.when(used)
    def _():
        x = jnp.concatenate([p.astype(BF16) for p in _load_rows(xs_ref, eb, nck)], axis=1)
        gu = _dot(x, w1b_ref[...]) + b1_ref[0, 0]
        g_h = jnp.minimum(gu[:, :f], SWIGLU_LIMIT)
        u_h = jnp.clip(gu[:, f:], -SWIGLU_LIMIT, SWIGLU_LIMIT)
        act = (u_h + 1.0) * (g_h * jax.nn.sigmoid(SWIGLU_ALPHA * g_h))
        _store_rows(ys_ref, _dot(act.astype(BF16), w2b_ref[...]) + b2_ref[0, 0], eb, nck)

    @pl.when(jnp.logical_not(used))
    def _():
        ys_ref[...] = jnp.zeros_like(ys_ref)


def _experts(xs, block_expert, n_used, layer, w1, b1, w2, b2, eb):
    depth, n_exp, d, f2 = w1.shape
    f = f2 // 2
    nck = d // LANES
    blk = (eb * nck, LANES)
    per_expert = lambda *tail: pl.BlockSpec((1, 1, *tail), lambda i, be, nu: (layer, be[i], 0, 0))
    grid_spec = pltpu.PrefetchScalarGridSpec(
        num_scalar_prefetch=2,
        grid=(xs.shape[0] // blk[0],),
        in_specs=[pl.BlockSpec(blk, lambda i, be, nu: (jnp.minimum(i, nu[0] - 1), 0)),
                  per_expert(d, f2), per_expert(1, f2), per_expert(f, d), per_expert(1, d)],
        out_specs=pl.BlockSpec(blk, lambda i, be, nu: (i, 0)),
        scratch_shapes=[pltpu.VMEM((d, f2), BF16), pltpu.VMEM((f, d), BF16)],
    )
    return pl.pallas_call(
        functools.partial(_expert_kernel, eb=eb, f=f, nck=nck),
        grid_spec=grid_spec,
        out_shape=jax.ShapeDtypeStruct(xs.shape, F32),
        compiler_params=_params("arbitrary"),
        name="moe_experts",
    )(block_expert, n_used, xs, w1, b1.reshape(depth, n_exp, 1, f2), w2, b2.reshape(depth, n_exp, 1, d))


def _combine_kernel(dest_ref, dest_next_ref, x_ref, gate_ref, mod_ref, ys_ref, o_ref, buf, sems,
                    *, tc, nck):
    i = pl.program_id(0)
    last = pl.num_programs(0) - 1
    cur = i % 2
    groups = tc // COMBINE_GROUP

    def issue_group(slots_ref, half, tg):
        for u in range(COMBINE_GROUP):
            t = tg * COMBINE_GROUP + u
            for k in range(TOP_K):
                slot = slots_ref[0, 0, t * TOP_K + k]
                src = ys_ref.at[pl.ds(pl.multiple_of(slot * nck, nck), nck)]
                dst = buf.at[half, k, pl.ds(pl.multiple_of(t * nck, nck), nck)]
                pltpu.make_async_copy(src, dst, sems.at[half]).start(priority=k % 2)

    def sum_group(tg):
        r0 = pl.multiple_of(tg * COMBINE_GROUP, COMBINE_GROUP)
        rows = pl.ds(r0, COMBINE_GROUP)
        g = gate_ref[rows, :]
        parts = [_load_rows(buf, COMBINE_GROUP, nck, lead=(cur, k), first=r0) for k in range(TOP_K)]
        for c in range(nck):
            lanes = slice(c * LANES, (c + 1) * LANES)
            moe = ((g[:, 0:1] * parts[0][c] + g[:, 1:2] * parts[1][c])
                   + (g[:, 2:3] * parts[2][c] + g[:, 3:4] * parts[3][c]))
            o_ref[rows, lanes] = x_ref[rows, lanes] + mod_ref[0, 5:6, lanes] * moe

    def loop(body):
        def step(tg, c):
            body(tg)
            return c
        lax.fori_loop(0, groups, step, 0)

    @pl.when(i == 0)
    def _():
        loop(lambda tg: issue_group(dest_ref, 0, tg))

    for k in range(TOP_K):
        pltpu.make_async_copy(ys_ref.at[pl.ds(0, tc * nck)], buf.at[cur, k], sems.at[cur]).wait()

    @pl.when(i < last)
    def _():
        loop(lambda tg: (issue_group(dest_next_ref, 1 - cur, tg), sum_group(tg)))

    @pl.when(i == last)
    def _():
        loop(sum_group)


def _combine(x2, gates_tok, dest_tok, mod, ys, seq, tc):
    n, d = x2.shape
    nck = d // LANES
    per_b = seq // tc
    n_tiles = n // tc
    slots = dest_tok.reshape(n_tiles, 1, tc * TOP_K)
    return pl.pallas_call(
        functools.partial(_combine_kernel, tc=tc, nck=nck),
        grid=(n_tiles,),
        in_specs=[pl.BlockSpec((1, 1, tc * TOP_K), lambda i: (i, 0, 0), memory_space=pltpu.SMEM),
                  pl.BlockSpec((1, 1, tc * TOP_K), lambda i: (jnp.minimum(i + 1, n_tiles - 1), 0, 0),
                               memory_space=pltpu.SMEM),
                  pl.BlockSpec((tc, d), lambda i: (i, 0)),
                  pl.BlockSpec((tc, TOP_K), lambda i: (i, 0)),
                  pl.BlockSpec((1, 6, d), lambda i: (i // per_b, 0, 0)),
                  pl.BlockSpec(memory_space=pl.ANY)],
        out_specs=pl.BlockSpec((tc, d), lambda i: (i, 0)),
        out_shape=jax.ShapeDtypeStruct((n, d), F32),
        scratch_shapes=[pltpu.VMEM((2, TOP_K, tc * nck, LANES), F32), pltpu.SemaphoreType.DMA((2,))],
        compiler_params=_params("arbitrary"),
        name="moe_combine",
    )(slots, slots, x2, gates_tok, mod, ys)


def _moe(x, routed, mod, layer, w1, b1, w2, b2, tiles):
    b, s, d = x.shape
    n = b * s
    h, idx, gates, rank, cnt = routed
    n_exp = cnt.shape[0]
    eb = tiles["eb"]
    x2 = x.reshape(n, d)
    counts = cnt[:, 0]
    padded = (counts + eb - 1) // eb * eb
    pad_end = jnp.cumsum(padded)
    pad_start = pad_end - padded
    n_blocks = n * TOP_K // eb + n_exp
    is_e = idx[None] == jnp.arange(n_exp, dtype=jnp.int32)[:, None, None]
    dest = rank + jnp.sum(jnp.where(is_e, pad_start[:, None, None], 0), axis=0)
    dest_tok = dest.T.reshape(-1)
    block_row0 = jnp.arange(n_blocks, dtype=jnp.int32) * eb
    block_expert = jnp.minimum(jnp.sum(pad_end[None, :] <= block_row0[:, None], axis=1),
                               n_exp - 1).astype(jnp.int32)
    n_used = (pad_end[-1:] // eb).astype(jnp.int32)
    fill_start = pad_start + counts
    xs = _dispatch(h, dest_tok, fill_start, pad_end - fill_start, n_used, n_blocks, eb, tiles["td"])
    ys = _experts(xs, block_expert, n_used, layer, w1, b1, w2, b2, eb)
    out = _combine(x2, gates.T, dest_tok, mod, ys, s, tiles["tc"])
    return out.reshape(b, s, d)


def _tiles(seq):
    tq = min(seq, 256)
    return {"ts": min(seq, 512), "tq": tq, "nq": min(seq // tq, 4),
            "td": min(seq, 1024), "tc": min(seq, 256), "eb": 512}


def kernel(x, c, norm1_g, norm2_g, ada_w, ada_b, conv_w_in, conv_w, conv_w_out, attn_w_qkv, attn_q_g, attn_k_g, attn_w_o, router_w, router_b, moe_w1, moe_b1, moe_w2, moe_b2):
    depth = ada_w.shape[0]
    t = _tiles(x.shape[1])
    mods = _ada(c, ada_w, ada_b)
    for i in range(depth):
        mod = mods[i]
        g1 = norm1_g[i][None, :]
        route = (norm2_g[i][None, :], router_w[i], router_b[i])
        j = i // 2
        if i % 2 == 0:
            x, *routed = _conv_mixer(x, mod, g1, conv_w_in[j].astype(BF16), conv_w[j],
                                     conv_w_out[j].astype(BF16), route, t["ts"])
        else:
            q, k, v = _qkv(x, mod, g1, attn_w_qkv[j].astype(BF16), attn_q_g[j], attn_k_g[j], t["ts"])
            o = _attention(q, k, v, t["tq"], t["nq"])
            x, *routed = _oproj(x, o, mod, attn_w_o[j].astype(BF16), route, t["ts"])
        x = _moe(x, routed, mod, i, moe_w1, moe_b1, moe_w2, moe_b2, t)
    return x
```

```python
import functools

import jax
import jax.numpy as jnp
from jax import lax
from jax.experimental import pallas as pl
from jax.experimental.pallas import tpu as pltpu

HEAD_DIM = 64
TOP_K = 4
CONV_W = 3
EPS = 1e-6
SWIGLU_LIMIT = 7.0
SWIGLU_ALPHA = 1.702
LANES = 128
SUBLANES = 8
HEADS_PER_TILE = LANES // HEAD_DIM
NEG_CUTOFF = -106.0
VMEM_LIMIT = 56 * 1024 * 1024
ISSUE_UNROLL = 4

F32 = jnp.float32
BF16 = jnp.bfloat16
_NT = (((1,), (1,)), ((), ()))


def _dot(a, b):
    return jnp.dot(a, b, preferred_element_type=F32)


def _split(a):
    hi = a.astype(BF16)
    lo = (a - hi.astype(F32)).astype(BF16)
    return hi, lo


def _modulate(x, g, sc, sh):
    ms = jnp.mean(x * x, axis=-1, keepdims=True)
    return (x * lax.rsqrt(ms + EPS)) * g * (1.0 + sc) + sh


def _load_rows(ref, n_rows, n_chunks, lead=(), first=0):
    return [ref[(*lead, pl.ds(first * n_chunks + c, n_rows, stride=n_chunks), slice(None))]
            for c in range(n_chunks)]


def _store_rows(ref, value, n_rows, n_chunks, first=0):
    for c in range(n_chunks):
        ref[pl.ds(first * n_chunks + c, n_rows, stride=n_chunks), :] = value[:, c * LANES:(c + 1) * LANES]


def _params(*sem):
    return pltpu.CompilerParams(dimension_semantics=sem, vmem_limit_bytes=VMEM_LIMIT)


def _ada_kernel(c_ref, w_ref, b_ref, o_ref):
    c = c_ref[...]
    ca = c * jax.nn.sigmoid(c)
    o_ref[0] = jnp.dot(ca, w_ref[0], preferred_element_type=F32,
                       precision=lax.Precision.HIGHEST) + b_ref[0]


def _ada(c, ada_w, ada_b):
    depth, d, d6 = ada_w.shape
    b = c.shape[0]
    tn = min(d6, 1536)
    out = pl.pallas_call(
        _ada_kernel,
        grid=(depth, d6 // tn),
        in_specs=[pl.BlockSpec((b, d), lambda i, j: (0, 0)),
                  pl.BlockSpec((1, d, tn), lambda i, j: (i, 0, j)),
                  pl.BlockSpec((1, 1, tn), lambda i, j: (i, 0, j))],
        out_specs=pl.BlockSpec((1, b, tn), lambda i, j: (i, 0, j)),
        out_shape=jax.ShapeDtypeStruct((depth, b, d6), F32),
        compiler_params=_params("parallel", "parallel"),
        name="ada",
    )(c, ada_w, ada_b.reshape(depth, 1, d6))
    return out.reshape(depth, b, 6, d)


def _route_tile(x, first, mod_ref, g_ref, rwh_ref, rwl_ref, rb_ref, tri_ref,
                h_ref, idx_ref, gate_ref, rank_ref, cnt_ref, base_ref):
    n_exp = rwh_ref.shape[0]
    tr = x.shape[0]

    @pl.when(first)
    def _():
        base_ref[...] = jnp.zeros_like(base_ref)

    h = _modulate(x, g_ref[...], mod_ref[0, 4:5, :], mod_ref[0, 3:4, :])
    _store_rows(h_ref, h, tr, h.shape[1] // LANES)
    hh, hl = _split(h)
    rwh = rwh_ref[...]
    logits = (lax.dot_general(rwh, hh, _NT, preferred_element_type=F32)
              + lax.dot_general(rwh, hl, _NT, preferred_element_type=F32)
              + lax.dot_general(rwl_ref[...], hh, _NT, preferred_element_type=F32)) + rb_ref[...]
    eio = lax.broadcasted_iota(jnp.int32, (n_exp, tr), 0)
    work = logits
    tops, onehots = [], []
    for k in range(TOP_K):
        m = jnp.max(work, axis=0, keepdims=True)
        am = jnp.min(jnp.where(work == m, eio, n_exp), axis=0, keepdims=True)
        oh = eio == am
        work = jnp.where(oh, -jnp.inf, work)
        tops.append(m)
        onehots.append(oh)
        idx_ref[k:k + 1, :] = am
    es = [jnp.exp(m - tops[0]) for m in tops]
    den = (es[0] + es[1]) + (es[2] + es[3])
    for k in range(TOP_K):
        gate_ref[k:k + 1, :] = es[k] / den
    sel = jnp.zeros((n_exp, tr), F32)
    for oh in onehots:
        sel = sel + oh.astype(F32)
    cum = _dot(sel.astype(BF16), tri_ref[...])
    base = base_ref[...]
    pos = cum[:, :tr] + jnp.concatenate([base] * (tr // LANES), axis=1)
    for k in range(TOP_K):
        rank_ref[k:k + 1, :] = jnp.sum(jnp.where(onehots[k], pos, 0.0), axis=0,
                                       keepdims=True).astype(jnp.int32)
    base = base + cum[:, tr:]
    base_ref[...] = base
    cnt_ref[...] = base.astype(jnp.int32)


N_ROUTE_IN, N_ROUTE_OUT = 5, 5


def _route_operands(g2, router_w, router_b, n, d, tr, tiles_per_batch):
    n_exp = router_w.shape[1]
    rwh, rwl = _split(router_w.T)
    jj = jnp.arange(tr)
    tri = (jj[:, None] < jj[None, :]).astype(BF16)
    text = jnp.concatenate([tri, jnp.ones((tr, LANES), BF16)], axis=1)
    full = lambda shape: pl.BlockSpec(shape, lambda i, j: (0,) * len(shape))
    tile = lambda i, j: i * tiles_per_batch + j
    row4 = pl.BlockSpec((TOP_K, tr), lambda i, j: (0, tile(i, j)))
    arrays = [g2, rwh, rwl, router_b[:, None], text]
    in_specs = [full((1, d)), full((n_exp, d)), full((n_exp, d)), full((n_exp, 1)),
                full((tr, tr + LANES))]
    out_specs = [pl.BlockSpec((tr * (d // LANES), LANES), lambda i, j: (tile(i, j), 0)),
                 row4, row4, row4, full((n_exp, LANES))]
    out_shape = [jax.ShapeDtypeStruct((n * (d // LANES), LANES), F32),
                 jax.ShapeDtypeStruct((TOP_K, n), jnp.int32),
                 jax.ShapeDtypeStruct((TOP_K, n), F32),
                 jax.ShapeDtypeStruct((TOP_K, n), jnp.int32),
                 jax.ShapeDtypeStruct((n_exp, LANES), jnp.int32)]
    scratch = [pltpu.VMEM((n_exp, LANES), F32)]
    return arrays, in_specs, out_specs, out_shape, scratch


def _conv_kernel(x_ref, mod_ref, g_ref, win_ref, cw_ref, wout_ref, *rest, ts, d):
    route_in, (o_ref, *route_out), (carry_ref, base_ref) = (
        rest[:N_ROUTE_IN], rest[N_ROUTE_IN:N_ROUTE_IN + 1 + N_ROUTE_OUT], rest[N_ROUTE_IN + 1 + N_ROUTE_OUT:])
    b, s = pl.program_id(0), pl.program_id(1)

    @pl.when(s == 0)
    def _():
        carry_ref[...] = jnp.zeros_like(carry_ref)

    x = x_ref[0]
    h = _modulate(x, g_ref[...], mod_ref[0, 1:2, :], mod_ref[0, 0:1, :]).astype(BF16)
    proj = _dot(h, win_ref[...])
    v = proj[:, d:2 * d] * proj[:, 2 * d:]
    prev = carry_ref[...]
    row = lax.broadcasted_iota(jnp.int32, (ts, 1), 0)
    v1 = jnp.where(row == 0, prev[7:8, :], pltpu.roll(v, 1, axis=0))
    v2 = jnp.where(row == 0, prev[6:7, :],
                   jnp.where(row == 1, prev[7:8, :], pltpu.roll(v, 2, axis=0)))
    y = cw_ref[2:3, :] * v + cw_ref[1:2, :] * v1 + cw_ref[0:1, :] * v2
    carry_ref[...] = v[ts - 8:, :]
    out = _dot((proj[:, :d] * y).astype(BF16), wout_ref[...])
    x_new = x + mod_ref[0, 2:3, :] * out
    o_ref[0] = x_new
    _route_tile(x_new, jnp.logical_and(b == 0, s == 0), mod_ref, *route_in, *route_out, base_ref)


def _conv_mixer(x, mod, g1, w_in, conv_w, w_out, route, ts):
    b, s, d = x.shape
    r_arrays, r_in, r_out, r_shape, r_scratch = _route_operands(*route, b * s, d, ts, s // ts)
    return pl.pallas_call(
        functools.partial(_conv_kernel, ts=ts, d=d),
        grid=(b, s // ts),
        in_specs=[pl.BlockSpec((1, ts, d), lambda i, j: (i, j, 0)),
                  pl.BlockSpec((1, 6, d), lambda i, j: (i, 0, 0)),
                  pl.BlockSpec((1, d), lambda i, j: (0, 0)),
                  pl.BlockSpec((d, 3 * d), lambda i, j: (0, 0)),
                  pl.BlockSpec((CONV_W, d), lambda i, j: (0, 0)),
                  pl.BlockSpec((d, d), lambda i, j: (0, 0)), *r_in],
        out_specs=[pl.BlockSpec((1, ts, d), lambda i, j: (i, j, 0)), *r_out],
        out_shape=[jax.ShapeDtypeStruct((b, s, d), F32), *r_shape],
        scratch_shapes=[pltpu.VMEM((8, d), F32), *r_scratch],
        compiler_params=_params("arbitrary", "arbitrary"),
        name="conv_mixer",
    )(x, mod, g1, w_in, conv_w, w_out, *r_arrays)


def _head_rms(t, gsum_ref, gexp_ref):
    ss = _dot((t * t).astype(BF16), gsum_ref[...])
    inv = lax.rsqrt(ss * (1.0 / HEAD_DIM) + EPS)
    ihi, ilo = _split(inv)
    return _dot(ihi, gexp_ref[...]) + _dot(ilo, gexp_ref[...])


def _qkv_kernel(x_ref, mod_ref, g_ref, w_ref, qg_ref, kg_ref, gsum_ref, gexp_ref,
                q_ref, k_ref, v_ref, *, d):
    x = x_ref[0]
    h = _modulate(x, g_ref[...], mod_ref[0, 1:2, :], mod_ref[0, 0:1, :]).astype(BF16)
    qkv = _dot(h, w_ref[...])
    q = qkv[:, :d]
    k = qkv[:, d:2 * d]
    q_ref[0] = (q * _head_rms(q, gsum_ref, gexp_ref) * (qg_ref[...] * HEAD_DIM ** -0.5)).astype(BF16)
    k_ref[0] = (k * _head_rms(k, gsum_ref, gexp_ref) * kg_ref[...]).astype(BF16)
    v_ref[0] = qkv[:, 2 * d:].astype(BF16)


def _qkv(x, mod, g1, w_qkv, q_g, k_g, ts):
    b, s, d = x.shape
    n_heads = d // HEAD_DIM
    head_of = jnp.arange(d) // HEAD_DIM
    gsum = (head_of[:, None] == jnp.arange(LANES)[None, :]).astype(BF16)
    gexp = gsum.T
    qg = jnp.tile(q_g, n_heads)[None, :]
    kg = jnp.tile(k_g, n_heads)[None, :]
    blk = pl.BlockSpec((1, ts, d), lambda i, j: (i, j, 0))
    full = lambda shape: pl.BlockSpec(shape, lambda i, j: (0,) * len(shape))
    return pl.pallas_call(
        functools.partial(_qkv_kernel, d=d),
        grid=(b, s // ts),
        in_specs=[blk, pl.BlockSpec((1, 6, d), lambda i, j: (i, 0, 0)), full((1, d)),
                  full((d, 3 * d)), full((1, d)), full((1, d)), full((d, LANES)), full((LANES, d))],
        out_specs=[blk, blk, blk],
        out_shape=[jax.ShapeDtypeStruct((b, s, d), BF16)] * 3,
        compiler_params=_params("parallel", "parallel"),
        name="qkv",
    )(x, mod, g1, w_qkv, qg, kg, gsum, gexp)


def _attn_kernel(q_ref, k_ref, v_ref, t_ref, o_ref, *, tq, nq):
    first_head = lax.broadcasted_iota(jnp.int32, (1, LANES), 1) < HEAD_DIM
    rows = HEADS_PER_TILE * tq
    row = lax.broadcasted_iota(jnp.int32, (rows, tq), 0)
    col = lax.broadcasted_iota(jnp.int32, (rows, tq), 1)
    causal = col < jnp.where(row >= tq, row - tq, row)

    def scores(q2, kb):
        start = pl.multiple_of(kb * tq, tq)
        z = lax.dot_general(q2, k_ref[0, pl.ds(start, tq), :], _NT, preferred_element_type=F32)
        return z, v_ref[0, pl.ds(start, tq), :]

    def logs(z, masked):
        log_beta = jnp.minimum(z, 0.0) - jnp.log(1.0 + jnp.exp(-jnp.abs(z)))
        l1m = log_beta - z
        return log_beta, (jnp.where(causal, l1m, 0.0) if masked else l1m)

    def row_total(suf, l1m):
        return suf[:, 0:1] + l1m[:, 0:1]

    q2s, qis, state = [], [], []
    for u in range(nq):
        qi = pl.program_id(2) * nq + u
        q = q_ref[0, u * tq:(u + 1) * tq, :]
        zero = jnp.zeros_like(q)
        q2 = jnp.concatenate([jnp.where(first_head, q, zero), jnp.where(first_head, zero, q)], axis=0)
        z_d, v_d = scores(q2, qi)
        z_p, v_p = scores(q2, jnp.maximum(qi - 1, 0))
        has_prev = (qi > 0).astype(F32)
        lb_d, lm_d = logs(z_d, True)
        lb_p, lm_p = logs(z_p, False)
        suf_d = _dot(lm_d.astype(BF16), t_ref[...])
        suf_p = _dot(lm_p.astype(BF16), t_ref[...])
        tot_d = row_total(suf_d, lm_d)
        a_d = jnp.where(causal, jnp.exp(lb_d + suf_d), 0.0)
        a_p = jnp.exp(lb_p + suf_p + tot_d)
        q2s.append(q2)
        qis.append(qi)
        state += [tot_d + has_prev * row_total(suf_p, lm_p),
                  _dot(a_d.astype(BF16), v_d) + has_prev * _dot(a_p.astype(BF16), v_p)]

    def cond(st):
        n = st[0]
        alive = [jnp.logical_and(qis[u] - 2 - n >= 0, jnp.max(st[1 + 2 * u]) > NEG_CUTOFF)
                 for u in range(nq)]
        return functools.reduce(jnp.logical_or, alive)

    def body(st):
        n = st[0]
        out = [n + 1]
        for u in range(nq):
            carry, acc = st[1 + 2 * u], st[2 + 2 * u]
            kb = qis[u] - 2 - n
            valid = (kb >= 0).astype(F32)
            z, vblk = scores(q2s[u], jnp.maximum(kb, 0))
            log_beta, l1m = logs(z, False)
            suf = _dot(l1m.astype(BF16), t_ref[...])
            a = jnp.exp(log_beta + suf + carry)
            out += [carry + valid * row_total(suf, l1m), acc + valid * _dot(a.astype(BF16), vblk)]
        return tuple(out)

    final = lax.while_loop(cond, body, (jnp.int32(0), *state))
    for u in range(nq):
        acc = final[2 + 2 * u]
        o_ref[0, u * tq:(u + 1) * tq, :] = jnp.where(first_head, acc[:tq], acc[tq:]).astype(o_ref.dtype)


def _attention(q, k, v, tq, nq):
    b, s, d = q.shape
    jj = jnp.arange(tq)
    text = (jj[:, None] > jj[None, :]).astype(BF16)
    return pl.pallas_call(
        functools.partial(_attn_kernel, tq=tq, nq=nq),
        grid=(b, d // LANES, s // (tq * nq)),
        in_specs=[pl.BlockSpec((1, tq * nq, LANES), lambda i, h, j: (i, j, h)),
                  pl.BlockSpec((1, s, LANES), lambda i, h, j: (i, 0, h)),
                  pl.BlockSpec((1, s, LANES), lambda i, h, j: (i, 0, h)),
                  pl.BlockSpec((tq, tq), lambda i, h, j: (0, 0))],
        out_specs=pl.BlockSpec((1, tq * nq, LANES), lambda i, h, j: (i, j, h)),
        out_shape=jax.ShapeDtypeStruct((b, s, d), BF16),
        compiler_params=_params("parallel", "parallel", "parallel"),
        name="stickbreak_attn",
    )(q, k, v, text)


def _oproj_kernel(x_ref, o_ref_in, mod_ref, w_ref, *rest):
    route_in, (out_ref, *route_out), (base_ref,) = (
        rest[:N_ROUTE_IN], rest[N_ROUTE_IN:N_ROUTE_IN + 1 + N_ROUTE_OUT], rest[N_ROUTE_IN + 1 + N_ROUTE_OUT:])
    x_new = x_ref[0] + mod_ref[0, 2:3, :] * _dot(o_ref_in[0], w_ref[...])
    out_ref[0] = x_new
    first = jnp.logical_and(pl.program_id(0) == 0, pl.program_id(1) == 0)
    _route_tile(x_new, first, mod_ref, *route_in, *route_out, base_ref)


def _oproj(x, o, mod, w_o, route, ts):
    b, s, d = x.shape
    blk = pl.BlockSpec((1, ts, d), lambda i, j: (i, j, 0))
    r_arrays, r_in, r_out, r_shape, r_scratch = _route_operands(*route, b * s, d, ts, s // ts)
    return pl.pallas_call(
        _oproj_kernel,
        grid=(b, s // ts),
        in_specs=[blk, blk, pl.BlockSpec((1, 6, d), lambda i, j: (i, 0, 0)),
                  pl.BlockSpec((d, d), lambda i, j: (0, 0)), *r_in],
        out_specs=[blk, *r_out],
        out_shape=[jax.ShapeDtypeStruct((b, s, d), F32), *r_shape],
        scratch_shapes=r_scratch,
        compiler_params=_params("arbitrary", "arbitrary"),
        name="attn_oproj",
    )(x, o, mod, w_o, *r_arrays)


def _zero_fill(fs_ref, fl_ref, nu_ref, zero_ref, xs_ref, zsem, *, eb, nck, n_exp, n_blocks, wait):
    def go(copy):
        if wait:
            copy.wait()
        else:
            copy.start()

    def rows(ref, first, count):
        return ref.at[pl.ds(pl.multiple_of(first * nck, nck), count * nck)]

    def per_expert(e, c):
        off = fs_ref[e]
        length = fl_ref[e]
        for bit in range(eb.bit_length() - 1):
            size = 1 << bit

            @pl.when((length & size) != 0)
            def _(off=off, size=size):
                go(pltpu.make_async_copy(rows(zero_ref, 0, size), rows(xs_ref, off, size), zsem))

            off = off + (length & size)
        return c

    lax.fori_loop(0, n_exp, per_expert, 0)

    def per_tail(j, c):
        blk = nu_ref[0] + j

        @pl.when(blk < n_blocks)
        def _():
            go(pltpu.make_async_copy(zero_ref, rows(xs_ref, blk * eb, eb), zsem))

        return c

    lax.fori_loop(0, n_exp, per_tail, 0)


def _dispatch_kernel(fs_ref, fl_ref, nu_ref, dest_ref, h_ref, xs_ref, zero_ref, sem, zsem,
                     *, td, eb, nck, n_exp, n_blocks):
    i = pl.program_id(0)
    fill = functools.partial(_zero_fill, fs_ref, fl_ref, nu_ref, zero_ref, xs_ref, zsem,
                             eb=eb, nck=nck, n_exp=n_exp, n_blocks=n_blocks)

    @pl.when(i == 0)
    def _():
        zero_ref[...] = jnp.zeros_like(zero_ref)
        fill(wait=False)

    def issue(t2, c):
        for u in range(ISSUE_UNROLL):
            t = t2 * ISSUE_UNROLL + u
            src = h_ref.at[pl.ds(pl.multiple_of(t * nck, nck), nck)]
            for k in range(TOP_K):
                slot = dest_ref[0, 0, t * TOP_K + k]
                dst = xs_ref.at[pl.ds(pl.multiple_of(slot * nck, nck), nck)]
                pltpu.make_async_copy(src, dst, sem).start(priority=k % 2)
        return c

    lax.fori_loop(0, td // ISSUE_UNROLL, issue, 0)
    for _ in range(TOP_K):
        pltpu.make_async_copy(h_ref, xs_ref.at[pl.ds(0, td * nck)], sem).wait()

    @pl.when(i == pl.num_programs(0) - 1)
    def _():
        fill(wait=True)


def _dispatch(h, dest_tok, fill_start, fill_len, n_used, n_blocks, eb, td):
    nck = h.shape[0] * TOP_K // dest_tok.shape[0]
    n = h.shape[0] // nck
    n_exp = fill_start.shape[0]
    grid_spec = pltpu.PrefetchScalarGridSpec(
        num_scalar_prefetch=3,
        grid=(n // td,),
        in_specs=[pl.BlockSpec((1, 1, td * TOP_K), lambda i, *_: (i, 0, 0), memory_space=pltpu.SMEM),
                  pl.BlockSpec((td * nck, LANES), lambda i, *_: (i, 0))],
        out_specs=pl.BlockSpec(memory_space=pl.ANY),
        scratch_shapes=[pltpu.VMEM((eb * nck, LANES), F32), pltpu.SemaphoreType.DMA,
                        pltpu.SemaphoreType.DMA],
    )
    return pl.pallas_call(
        functools.partial(_dispatch_kernel, td=td, eb=eb, nck=nck, n_exp=n_exp, n_blocks=n_blocks),
        grid_spec=grid_spec,
        out_shape=jax.ShapeDtypeStruct((n_blocks * eb * nck, LANES), F32),
        compiler_params=_params("arbitrary"),
        name="moe_dispatch",
    )(fill_start, fill_len, n_used, dest_tok.reshape(n // td, 1, td * TOP_K), h)


def _expert_kernel(be_ref, nu_ref, xs_ref, w1_ref, b1_ref, w2_ref, b2_ref, ys_ref, w1b_ref, w2b_ref,
                   *, eb, f, nck):
    i = pl.program_id(0)
    used = i < nu_ref[0]
    new_expert = jnp.logical_or(i == 0, be_ref[i] != be_ref[jnp.maximum(i - 1, 0)])

    @pl.when(jnp.logical_and(used, new_expert))
    def _():
        w1b_ref[...] = w1_ref[0, 0].astype(BF16)
        w2b_ref[...] = w2_ref[0, 0].astype(BF16)

    @pl.when(used)
    def _():
        x = jnp.concatenate([p.astype(BF16) for p in _load_rows(xs_ref, eb, nck)], axis=1)
        gu = _dot(x, w1b_ref[...]) + b1_ref[0, 0]
        g_h = jnp.minimum(gu[:, :f], SWIGLU_LIMIT)
        u_h = jnp.clip(gu[:, f:], -SWIGLU_LIMIT, SWIGLU_LIMIT)
        act = (u_h + 1.0) * (g_h * jax.nn.sigmoid(SWIGLU_ALPHA * g_h))
        _store_rows(ys_ref, _dot(act.astype(BF16), w2b_ref[...]) + b2_ref[0, 0], eb, nck)

    @pl.when(jnp.logical_not(used))
    def _():
        ys_ref[...] = jnp.zeros_like(ys_ref)


def _experts(xs, block_expert, n_used, layer, w1, b1, w2, b2, eb):
    depth, n_exp, d, f2 = w1.shape
    f = f2 // 2
    nck = d // LANES
    blk = (eb * nck, LANES)
    per_expert = lambda *tail: pl.BlockSpec((1, 1, *tail), lambda i, be, nu: (layer, be[i], 0, 0))
    grid_spec = pltpu.PrefetchScalarGridSpec(
        num_scalar_prefetch=2,
        grid=(xs.shape[0] // blk[0],),
        in_specs=[pl.BlockSpec(blk, lambda i, be, nu: (jnp.minimum(i, nu[0] - 1), 0)),
                  per_expert(d, f2), per_expert(1, f2), per_expert(f, d), per_expert(1, d)],
        out_specs=pl.BlockSpec(blk, lambda i, be, nu: (i, 0)),
        scratch_shapes=[pltpu.VMEM((d, f2), BF16), pltpu.VMEM((f, d), BF16)],
    )
    return pl.pallas_call(
        functools.partial(_expert_kernel, eb=eb, f=f, nck=nck),
        grid_spec=grid_spec,
        out_shape=jax.ShapeDtypeStruct(xs.shape, F32),
        compiler_params=_params("arbitrary"),
        name="moe_experts",
    )(block_expert, n_used, xs, w1, b1.reshape(depth, n_exp, 1, f2), w2, b2.reshape(depth, n_exp, 1, d))


def _combine_kernel(dest_ref, dest_next_ref, x_ref, gate_ref, mod_ref, ys_ref, o_ref, buf, sems,
                    *, tc, nck):
    i = pl.program_id(0)

    def gather(slots_ref, half):
        def issue(t2, c):
            for u in range(ISSUE_UNROLL):
                t = t2 * ISSUE_UNROLL + u
                for k in range(TOP_K):
                    slot = slots_ref[0, 0, t * TOP_K + k]
                    src = ys_ref.at[pl.ds(pl.multiple_of(slot * nck, nck), nck)]
                    dst = buf.at[half, k, pl.ds(pl.multiple_of(t * nck, nck), nck)]
                    pltpu.make_async_copy(src, dst, sems.at[half]).start(priority=k % 2)
            return c

        lax.fori_loop(0, tc // ISSUE_UNROLL, issue, 0)

    @pl.when(i == 0)
    def _():
        gather(dest_ref, 0)

    def step(cur):
        @pl.when(i + 1 < pl.num_programs(0))
        def _():
            gather(dest_next_ref, 1 - cur)

        for k in range(TOP_K):
            pltpu.make_async_copy(ys_ref.at[pl.ds(0, tc * nck)], buf.at[cur, k], sems.at[cur]).wait()
        g = gate_ref[...]
        parts = [_load_rows(buf, tc, nck, lead=(cur, k)) for k in range(TOP_K)]
        for c in range(nck):
            lanes = slice(c * LANES, (c + 1) * LANES)
            moe = ((g[:, 0:1] * parts[0][c] + g[:, 1:2] * parts[1][c])
                   + (g[:, 2:3] * parts[2][c] + g[:, 3:4] * parts[3][c]))
            o_ref[:, lanes] = x_ref[:, lanes] + mod_ref[0, 5:6, lanes] * moe

    for half in range(2):
        pl.when(i % 2 == half)(functools.partial(step, half))


def _combine(x2, gates_tok, dest_tok, mod, ys, seq, tc):
    n, d = x2.shape
    nck = d // LANES
    per_b = seq // tc
    n_tiles = n // tc
    slots = dest_tok.reshape(n_tiles, 1, tc * TOP_K)
    return pl.pallas_call(
        functools.partial(_combine_kernel, tc=tc, nck=nck),
        grid=(n_tiles,),
        in_specs=[pl.BlockSpec((1, 1, tc * TOP_K), lambda i: (i, 0, 0), memory_space=pltpu.SMEM),
                  pl.BlockSpec((1, 1, tc * TOP_K), lambda i: (jnp.minimum(i + 1, n_tiles - 1), 0, 0),
                               memory_space=pltpu.SMEM),
                  pl.BlockSpec((tc, d), lambda i: (i, 0)),
                  pl.BlockSpec((tc, TOP_K), lambda i: (i, 0)),
                  pl.BlockSpec((1, 6, d), lambda i: (i // per_b, 0, 0)),
                  pl.BlockSpec(memory_space=pl.ANY)],
        out_specs=pl.BlockSpec((tc, d), lambda i: (i, 0)),
        out_shape=jax.ShapeDtypeStruct((n, d), F32),
        scratch_shapes=[pltpu.VMEM((2, TOP_K, tc * nck, LANES), F32), pltpu.SemaphoreType.DMA((2,))],
        compiler_params=_params("arbitrary"),
        name="moe_combine",
    )(slots, slots, x2, gates_tok, mod, ys)


def _moe(x, routed, mod, layer, w1, b1, w2, b2, tiles):
    b, s, d = x.shape
    n = b * s
    h, idx, gates, rank, cnt = routed
    n_exp = cnt.shape[0]
    eb = tiles["eb"]
    x2 = x.reshape(n, d)
    counts = cnt[:, 0]
    padded = (counts + eb - 1) // eb * eb
    pad_end = jnp.cumsum(padded)
    pad_start = pad_end - padded
    n_blocks = n * TOP_K // eb + n_exp
    is_e = idx[None] == jnp.arange(n_exp, dtype=jnp.int32)[:, None, None]
    dest = rank + jnp.sum(jnp.where(is_e, pad_start[:, None, None], 0), axis=0)
    dest_tok = dest.T.reshape(-1)
    block_row0 = jnp.arange(n_blocks, dtype=jnp.int32) * eb
    block_expert = jnp.minimum(jnp.sum(pad_end[None, :] <= block_row0[:, None], axis=1),
                               n_exp - 1).astype(jnp.int32)
    n_used = (pad_end[-1:] // eb).astype(jnp.int32)
    fill_start = pad_start + counts
    xs = _dispatch(h, dest_tok, fill_start, pad_end - fill_start, n_used, n_blocks, eb, tiles["td"])
    ys = _experts(xs, block_expert, n_used, layer, w1, b1, w2, b2, eb)
    out = _combine(x2, gates.T, dest_tok, mod, ys, s, tiles["tc"])
    return out.reshape(b, s, d)


def _tiles(seq):
    tq = min(seq, 256)
    return {"ts": min(seq, 512), "tq": tq, "nq": min(seq // tq, 8),
            "td": min(seq, 1024), "tc": min(seq, 256), "eb": 512}


def kernel(x, c, norm1_g, norm2_g, ada_w, ada_b, conv_w_in, conv_w, conv_w_out, attn_w_qkv, attn_q_g, attn_k_g, attn_w_o, router_w, router_b, moe_w1, moe_b1, moe_w2, moe_b2):
    depth = ada_w.shape[0]
    t = _tiles(x.shape[1])
    mods = _ada(c, ada_w, ada_b)
    for i in range(depth):
        mod = mods[i]
        g1 = norm1_g[i][None, :]
        route = (norm2_g[i][None, :], router_w[i], router_b[i])
        j = i // 2
        if i % 2 == 0:
            x, *routed = _conv_mixer(x, mod, g1, conv_w_in[j].astype(BF16), conv_w[j],
                                     conv_w_out[j].astype(BF16), route, t["ts"])
        else:
            q, k, v = _qkv(x, mod, g1, attn_w_qkv[j].astype(BF16), attn_q_g[j], attn_k_g[j], t["ts"])
            o = _attention(q, k, v, t["tq"], t["nq"])
            x, *routed = _oproj(x, o, mod, attn_w_o[j].astype(BF16), route, t["ts"])
        x = _moe(x, routed, mod, i, moe_w1, moe_b1, moe_w2, moe_b2, t)
    return x
```

```python
import functools

import jax
import jax.numpy as jnp
from jax import lax
from jax.experimental import pallas as pl
from jax.experimental.pallas import tpu as pltpu

HEAD_DIM = 64
TOP_K = 4
CONV_W = 3
EPS = 1e-6
SWIGLU_LIMIT = 7.0
SWIGLU_ALPHA = 1.702
LANES = 128
SUBLANES = 8
HEADS_PER_TILE = LANES // HEAD_DIM
NEG_CUTOFF = -106.0
VMEM_LIMIT = 56 * 1024 * 1024
ISSUE_UNROLL = 8

F32 = jnp.float32
BF16 = jnp.bfloat16
_NT = (((1,), (1,)), ((), ()))


def _dot(a, b):
    return jnp.dot(a, b, preferred_element_type=F32)


def _split(a):
    hi = a.astype(BF16)
    lo = (a - hi.astype(F32)).astype(BF16)
    return hi, lo


def _modulate(x, g, sc, sh):
    ms = jnp.mean(x * x, axis=-1, keepdims=True)
    return (x * lax.rsqrt(ms + EPS)) * g * (1.0 + sc) + sh


def _load_rows(ref, n_rows, n_chunks, lead=(), first=0):
    return [ref[(*lead, pl.ds(first * n_chunks + c, n_rows, stride=n_chunks), slice(None))]
            for c in range(n_chunks)]


def _store_rows(ref, value, n_rows, n_chunks, first=0):
    for c in range(n_chunks):
        ref[pl.ds(first * n_chunks + c, n_rows, stride=n_chunks), :] = value[:, c * LANES:(c + 1) * LANES]


def _params(*sem):
    return pltpu.CompilerParams(dimension_semantics=sem, vmem_limit_bytes=VMEM_LIMIT)


def _ada_kernel(c_ref, w_ref, b_ref, o_ref):
    c = c_ref[...]
    ca = c * jax.nn.sigmoid(c)
    o_ref[0] = jnp.dot(ca, w_ref[0], preferred_element_type=F32,
                       precision=lax.Precision.HIGHEST) + b_ref[0]


def _ada(c, ada_w, ada_b):
    depth, d, d6 = ada_w.shape
    b = c.shape[0]
    tn = min(d6, 1536)
    out = pl.pallas_call(
        _ada_kernel,
        grid=(depth, d6 // tn),
        in_specs=[pl.BlockSpec((b, d), lambda i, j: (0, 0)),
                  pl.BlockSpec((1, d, tn), lambda i, j: (i, 0, j)),
                  pl.BlockSpec((1, 1, tn), lambda i, j: (i, 0, j))],
        out_specs=pl.BlockSpec((1, b, tn), lambda i, j: (i, 0, j)),
        out_shape=jax.ShapeDtypeStruct((depth, b, d6), F32),
        compiler_params=_params("parallel", "parallel"),
        name="ada",
    )(c, ada_w, ada_b.reshape(depth, 1, d6))
    return out.reshape(depth, b, 6, d)


def _route_tile(x, first, mod_ref, g_ref, rwh_ref, rwl_ref, rb_ref, tri_ref,
                h_ref, idx_ref, gate_ref, rank_ref, cnt_ref, base_ref):
    n_exp = rwh_ref.shape[0]
    tr = x.shape[0]

    @pl.when(first)
    def _():
        base_ref[...] = jnp.zeros_like(base_ref)

    h = _modulate(x, g_ref[...], mod_ref[0, 4:5, :], mod_ref[0, 3:4, :])
    _store_rows(h_ref, h, tr, h.shape[1] // LANES)
    hh, hl = _split(h)
    rwh = rwh_ref[...]
    logits = (lax.dot_general(rwh, hh, _NT, preferred_element_type=F32)
              + lax.dot_general(rwh, hl, _NT, preferred_element_type=F32)
              + lax.dot_general(rwl_ref[...], hh, _NT, preferred_element_type=F32)) + rb_ref[...]
    eio = lax.broadcasted_iota(jnp.int32, (n_exp, tr), 0)
    work = logits
    tops, onehots = [], []
    for k in range(TOP_K):
        m = jnp.max(work, axis=0, keepdims=True)
        am = jnp.min(jnp.where(work == m, eio, n_exp), axis=0, keepdims=True)
        oh = eio == am
        work = jnp.where(oh, -jnp.inf, work)
        tops.append(m)
        onehots.append(oh)
        idx_ref[k:k + 1, :] = am
    es = [jnp.exp(m - tops[0]) for m in tops]
    den = (es[0] + es[1]) + (es[2] + es[3])
    for k in range(TOP_K):
        gate_ref[k:k + 1, :] = es[k] / den
    sel = jnp.zeros((n_exp, tr), F32)
    for oh in onehots:
        sel = sel + oh.astype(F32)
    cum = _dot(sel.astype(BF16), tri_ref[...])
    base = base_ref[...]
    pos = cum[:, :tr] + jnp.concatenate([base] * (tr // LANES), axis=1)
    for k in range(TOP_K):
        rank_ref[k:k + 1, :] = jnp.sum(jnp.where(onehots[k], pos, 0.0), axis=0,
                                       keepdims=True).astype(jnp.int32)
    base = base + cum[:, tr:]
    base_ref[...] = base
    cnt_ref[...] = base.astype(jnp.int32)


N_ROUTE_IN, N_ROUTE_OUT = 5, 5


def _route_operands(g2, router_w, router_b, n, d, tr, tiles_per_batch):
    n_exp = router_w.shape[1]
    rwh, rwl = _split(router_w.T)
    jj = jnp.arange(tr)
    tri = (jj[:, None] < jj[None, :]).astype(BF16)
    text = jnp.concatenate([tri, jnp.ones((tr, LANES), BF16)], axis=1)
    full = lambda shape: pl.BlockSpec(shape, lambda i, j: (0,) * len(shape))
    tile = lambda i, j: i * tiles_per_batch + j
    row4 = pl.BlockSpec((TOP_K, tr), lambda i, j: (0, tile(i, j)))
    arrays = [g2, rwh, rwl, router_b[:, None], text]
    in_specs = [full((1, d)), full((n_exp, d)), full((n_exp, d)), full((n_exp, 1)),
                full((tr, tr + LANES))]
    out_specs = [pl.BlockSpec((tr * (d // LANES), LANES), lambda i, j: (tile(i, j), 0)),
                 row4, row4, row4, full((n_exp, LANES))]
    out_shape = [jax.ShapeDtypeStruct((n * (d // LANES), LANES), F32),
                 jax.ShapeDtypeStruct((TOP_K, n), jnp.int32),
                 jax.ShapeDtypeStruct((TOP_K, n), F32),
                 jax.ShapeDtypeStruct((TOP_K, n), jnp.int32),
                 jax.ShapeDtypeStruct((n_exp, LANES), jnp.int32)]
    scratch = [pltpu.VMEM((n_exp, LANES), F32)]
    return arrays, in_specs, out_specs, out_shape, scratch


def _conv_kernel(x_ref, mod_ref, g_ref, win_ref, cw_ref, wout_ref, *rest, ts, d):
    route_in, (o_ref, *route_out), (carry_ref, base_ref) = (
        rest[:N_ROUTE_IN], rest[N_ROUTE_IN:N_ROUTE_IN + 1 + N_ROUTE_OUT], rest[N_ROUTE_IN + 1 + N_ROUTE_OUT:])
    b, s = pl.program_id(0), pl.program_id(1)

    @pl.when(s == 0)
    def _():
        carry_ref[...] = jnp.zeros_like(carry_ref)

    x = x_ref[0]
    h = _modulate(x, g_ref[...], mod_ref[0, 1:2, :], mod_ref[0, 0:1, :]).astype(BF16)
    proj = _dot(h, win_ref[...])
    v = proj[:, d:2 * d] * proj[:, 2 * d:]
    prev = carry_ref[...]
    row = lax.broadcasted_iota(jnp.int32, (ts, 1), 0)
    v1 = jnp.where(row == 0, prev[7:8, :], pltpu.roll(v, 1, axis=0))
    v2 = jnp.where(row == 0, prev[6:7, :],
                   jnp.where(row == 1, prev[7:8, :], pltpu.roll(v, 2, axis=0)))
    y = cw_ref[2:3, :] * v + cw_ref[1:2, :] * v1 + cw_ref[0:1, :] * v2
    carry_ref[...] = v[ts - 8:, :]
    out = _dot((proj[:, :d] * y).astype(BF16), wout_ref[...])
    x_new = x + mod_ref[0, 2:3, :] * out
    o_ref[0] = x_new
    _route_tile(x_new, jnp.logical_and(b == 0, s == 0), mod_ref, *route_in, *route_out, base_ref)


def _conv_mixer(x, mod, g1, w_in, conv_w, w_out, route, ts):
    b, s, d = x.shape
    r_arrays, r_in, r_out, r_shape, r_scratch = _route_operands(*route, b * s, d, ts, s // ts)
    return pl.pallas_call(
        functools.partial(_conv_kernel, ts=ts, d=d),
        grid=(b, s // ts),
        in_specs=[pl.BlockSpec((1, ts, d), lambda i, j: (i, j, 0)),
                  pl.BlockSpec((1, 6, d), lambda i, j: (i, 0, 0)),
                  pl.BlockSpec((1, d), lambda i, j: (0, 0)),
                  pl.BlockSpec((d, 3 * d), lambda i, j: (0, 0)),
                  pl.BlockSpec((CONV_W, d), lambda i, j: (0, 0)),
                  pl.BlockSpec((d, d), lambda i, j: (0, 0)), *r_in],
        out_specs=[pl.BlockSpec((1, ts, d), lambda i, j: (i, j, 0)), *r_out],
        out_shape=[jax.ShapeDtypeStruct((b, s, d), F32), *r_shape],
        scratch_shapes=[pltpu.VMEM((8, d), F32), *r_scratch],
        compiler_params=_params("arbitrary", "arbitrary"),
        name="conv_mixer",
    )(x, mod, g1, w_in, conv_w, w_out, *r_arrays)


def _head_rms(t, gsum_ref, gexp_ref):
    ss = _dot((t * t).astype(BF16), gsum_ref[...])
    inv = lax.rsqrt(ss * (1.0 / HEAD_DIM) + EPS)
    ihi, ilo = _split(inv)
    return _dot(ihi, gexp_ref[...]) + _dot(ilo, gexp_ref[...])


def _qkv_kernel(x_ref, mod_ref, g_ref, w_ref, qg_ref, kg_ref, gsum_ref, gexp_ref,
                q_ref, k_ref, v_ref, *, d):
    x = x_ref[0]
    h = _modulate(x, g_ref[...], mod_ref[0, 1:2, :], mod_ref[0, 0:1, :]).astype(BF16)
    qkv = _dot(h, w_ref[...])
    q = qkv[:, :d]
    k = qkv[:, d:2 * d]
    q_ref[0] = (q * _head_rms(q, gsum_ref, gexp_ref) * (qg_ref[...] * HEAD_DIM ** -0.5)).astype(BF16)
    k_ref[0] = (k * _head_rms(k, gsum_ref, gexp_ref) * kg_ref[...]).astype(BF16)
    v_ref[0] = qkv[:, 2 * d:].astype(BF16)


def _qkv(x, mod, g1, w_qkv, q_g, k_g, ts):
    b, s, d = x.shape
    n_heads = d // HEAD_DIM
    head_of = jnp.arange(d) // HEAD_DIM
    gsum = (head_of[:, None] == jnp.arange(LANES)[None, :]).astype(BF16)
    gexp = gsum.T
    qg = jnp.tile(q_g, n_heads)[None, :]
    kg = jnp.tile(k_g, n_heads)[None, :]
    blk = pl.BlockSpec((1, ts, d), lambda i, j: (i, j, 0))
    full = lambda shape: pl.BlockSpec(shape, lambda i, j: (0,) * len(shape))
    return pl.pallas_call(
        functools.partial(_qkv_kernel, d=d),
        grid=(b, s // ts),
        in_specs=[blk, pl.BlockSpec((1, 6, d), lambda i, j: (i, 0, 0)), full((1, d)),
                  full((d, 3 * d)), full((1, d)), full((1, d)), full((d, LANES)), full((LANES, d))],
        out_specs=[blk, blk, blk],
        out_shape=[jax.ShapeDtypeStruct((b, s, d), BF16)] * 3,
        compiler_params=_params("parallel", "parallel"),
        name="qkv",
    )(x, mod, g1, w_qkv, qg, kg, gsum, gexp)


def _attn_kernel(q_ref, k_ref, v_ref, t_ref, o_ref, *, tq, nq):
    first_head = lax.broadcasted_iota(jnp.int32, (1, LANES), 1) < HEAD_DIM
    rows = HEADS_PER_TILE * tq
    row = lax.broadcasted_iota(jnp.int32, (rows, tq), 0)
    col = lax.broadcasted_iota(jnp.int32, (rows, tq), 1)
    causal = col < jnp.where(row >= tq, row - tq, row)

    def scores(q2, kb):
        start = pl.multiple_of(kb * tq, tq)
        z = lax.dot_general(q2, k_ref[0, pl.ds(start, tq), :], _NT, preferred_element_type=F32)
        return z, v_ref[0, pl.ds(start, tq), :]

    def logs(z, masked):
        log_beta = jnp.minimum(z, 0.0) - jnp.log(1.0 + jnp.exp(-jnp.abs(z)))
        l1m = log_beta - z
        return log_beta, (jnp.where(causal, l1m, 0.0) if masked else l1m)

    def row_total(suf, l1m):
        return suf[:, 0:1] + l1m[:, 0:1]

    q2s, qis, state = [], [], []
    for u in range(nq):
        qi = pl.program_id(2) * nq + u
        q = q_ref[0, u * tq:(u + 1) * tq, :]
        zero = jnp.zeros_like(q)
        q2 = jnp.concatenate([jnp.where(first_head, q, zero), jnp.where(first_head, zero, q)], axis=0)
        z_d, v_d = scores(q2, qi)
        z_p, v_p = scores(q2, jnp.maximum(qi - 1, 0))
        has_prev = (qi > 0).astype(F32)
        lb_d, lm_d = logs(z_d, True)
        lb_p, lm_p = logs(z_p, False)
        suf_d = _dot(lm_d.astype(BF16), t_ref[...])
        suf_p = _dot(lm_p.astype(BF16), t_ref[...])
        tot_d = row_total(suf_d, lm_d)
        a_d = jnp.where(causal, jnp.exp(lb_d + suf_d), 0.0)
        a_p = jnp.exp(lb_p + suf_p + tot_d)
        q2s.append(q2)
        qis.append(qi)
        state += [tot_d + has_prev * row_total(suf_p, lm_p),
                  _dot(a_d.astype(BF16), v_d) + has_prev * _dot(a_p.astype(BF16), v_p)]

    def cond(st):
        n = st[0]
        alive = [jnp.logical_and(qis[u] - 2 - n >= 0, jnp.max(st[1 + 2 * u]) > NEG_CUTOFF)
                 for u in range(nq)]
        return functools.reduce(jnp.logical_or, alive)

    def body(st):
        n = st[0]
        out = [n + 1]
        for u in range(nq):
            carry, acc = st[1 + 2 * u], st[2 + 2 * u]
            kb = qis[u] - 2 - n
            valid = (kb >= 0).astype(F32)
            z, vblk = scores(q2s[u], jnp.maximum(kb, 0))
            log_beta, l1m = logs(z, False)
            suf = _dot(l1m.astype(BF16), t_ref[...])
            a = jnp.exp(log_beta + suf + carry)
            out += [carry + valid * row_total(suf, l1m), acc + valid * _dot(a.astype(BF16), vblk)]
        return tuple(out)

    final = lax.while_loop(cond, body, (jnp.int32(0), *state))
    for u in range(nq):
        acc = final[2 + 2 * u]
        o_ref[0, u * tq:(u + 1) * tq, :] = jnp.where(first_head, acc[:tq], acc[tq:]).astype(o_ref.dtype)


def _attention(q, k, v, tq, nq):
    b, s, d = q.shape
    jj = jnp.arange(tq)
    text = (jj[:, None] > jj[None, :]).astype(BF16)
    return pl.pallas_call(
        functools.partial(_attn_kernel, tq=tq, nq=nq),
        grid=(b, d // LANES, s // (tq * nq)),
        in_specs=[pl.BlockSpec((1, tq * nq, LANES), lambda i, h, j: (i, j, h)),
                  pl.BlockSpec((1, s, LANES), lambda i, h, j: (i, 0, h)),
                  pl.BlockSpec((1, s, LANES), lambda i, h, j: (i, 0, h)),
                  pl.BlockSpec((tq, tq), lambda i, h, j: (0, 0))],
        out_specs=pl.BlockSpec((1, tq * nq, LANES), lambda i, h, j: (i, j, h)),
        out_shape=jax.ShapeDtypeStruct((b, s, d), BF16),
        compiler_params=_params("parallel", "parallel", "parallel"),
        name="stickbreak_attn",
    )(q, k, v, text)


def _oproj_kernel(x_ref, o_ref_in, mod_ref, w_ref, *rest):
    route_in, (out_ref, *route_out), (base_ref,) = (
        rest[:N_ROUTE_IN], rest[N_ROUTE_IN:N_ROUTE_IN + 1 + N_ROUTE_OUT], rest[N_ROUTE_IN + 1 + N_ROUTE_OUT:])
    x_new = x_ref[0] + mod_ref[0, 2:3, :] * _dot(o_ref_in[0], w_ref[...])
    out_ref[0] = x_new
    first = jnp.logical_and(pl.program_id(0) == 0, pl.program_id(1) == 0)
    _route_tile(x_new, first, mod_ref, *route_in, *route_out, base_ref)


def _oproj(x, o, mod, w_o, route, ts):
    b, s, d = x.shape
    blk = pl.BlockSpec((1, ts, d), lambda i, j: (i, j, 0))
    r_arrays, r_in, r_out, r_shape, r_scratch = _route_operands(*route, b * s, d, ts, s // ts)
    return pl.pallas_call(
        _oproj_kernel,
        grid=(b, s // ts),
        in_specs=[blk, blk, pl.BlockSpec((1, 6, d), lambda i, j: (i, 0, 0)),
                  pl.BlockSpec((d, d), lambda i, j: (0, 0)), *r_in],
        out_specs=[blk, *r_out],
        out_shape=[jax.ShapeDtypeStruct((b, s, d), F32), *r_shape],
        scratch_shapes=r_scratch,
        compiler_params=_params("arbitrary", "arbitrary"),
        name="attn_oproj",
    )(x, o, mod, w_o, *r_arrays)


def _zero_fill(fs_ref, fl_ref, nu_ref, zero_ref, xs_ref, zsem, *, eb, nck, n_exp, n_blocks, wait):
    def go(copy):
        if wait:
            copy.wait()
        else:
            copy.start()

    def rows(ref, first, count):
        return ref.at[pl.ds(pl.multiple_of(first * nck, nck), count * nck)]

    def per_expert(e, c):
        off = fs_ref[e]
        length = fl_ref[e]
        for bit in range(eb.bit_length() - 1):
            size = 1 << bit

            @pl.when((length & size) != 0)
            def _(off=off, size=size):
                go(pltpu.make_async_copy(rows(zero_ref, 0, size), rows(xs_ref, off, size), zsem))

            off = off + (length & size)
        return c

    lax.fori_loop(0, n_exp, per_expert, 0)

    def per_tail(j, c):
        blk = nu_ref[0] + j

        @pl.when(blk < n_blocks)
        def _():
            go(pltpu.make_async_copy(zero_ref, rows(xs_ref, blk * eb, eb), zsem))

        return c

    lax.fori_loop(0, n_exp, per_tail, 0)


def _dispatch_kernel(fs_ref, fl_ref, nu_ref, dest_ref, h_ref, xs_ref, zero_ref, sem, zsem,
                     *, td, eb, nck, n_exp, n_blocks):
    i = pl.program_id(0)
    fill = functools.partial(_zero_fill, fs_ref, fl_ref, nu_ref, zero_ref, xs_ref, zsem,
                             eb=eb, nck=nck, n_exp=n_exp, n_blocks=n_blocks)

    @pl.when(i == 0)
    def _():
        zero_ref[...] = jnp.zeros_like(zero_ref)
        fill(wait=False)

    def issue(t2, c):
        for u in range(ISSUE_UNROLL):
            t = t2 * ISSUE_UNROLL + u
            src = h_ref.at[pl.ds(pl.multiple_of(t * nck, nck), nck)]
            for k in range(TOP_K):
                slot = dest_ref[0, 0, t * TOP_K + k]
                dst = xs_ref.at[pl.ds(pl.multiple_of(slot * nck, nck), nck)]
                pltpu.make_async_copy(src, dst, sem).start(priority=k % 2)
        return c

    lax.fori_loop(0, td // ISSUE_UNROLL, issue, 0)
    for _ in range(TOP_K):
        pltpu.make_async_copy(h_ref, xs_ref.at[pl.ds(0, td * nck)], sem).wait()

    @pl.when(i == pl.num_programs(0) - 1)
    def _():
        fill(wait=True)


def _dispatch(h, dest_tok, fill_start, fill_len, n_used, n_blocks, eb, td):
    nck = h.shape[0] * TOP_K // dest_tok.shape[0]
    n = h.shape[0] // nck
    n_exp = fill_start.shape[0]
    grid_spec = pltpu.PrefetchScalarGridSpec(
        num_scalar_prefetch=3,
        grid=(n // td,),
        in_specs=[pl.BlockSpec((1, 1, td * TOP_K), lambda i, *_: (i, 0, 0), memory_space=pltpu.SMEM),
                  pl.BlockSpec((td * nck, LANES), lambda i, *_: (i, 0))],
        out_specs=pl.BlockSpec(memory_space=pl.ANY),
        scratch_shapes=[pltpu.VMEM((eb * nck, LANES), F32), pltpu.SemaphoreType.DMA,
                        pltpu.SemaphoreType.DMA],
    )
    return pl.pallas_call(
        functools.partial(_dispatch_kernel, td=td, eb=eb, nck=nck, n_exp=n_exp, n_blocks=n_blocks),
        grid_spec=grid_spec,
        out_shape=jax.ShapeDtypeStruct((n_blocks * eb * nck, LANES), F32),
        compiler_params=_params("arbitrary"),
        name="moe_dispatch",
    )(fill_start, fill_len, n_used, dest_tok.reshape(n // td, 1, td * TOP_K), h)


def _expert_kernel(be_ref, nu_ref, xs_ref, w1_ref, b1_ref, w2_ref, b2_ref, ys_ref, w1b_ref, w2b_ref,
                   *, eb, f, nck):
    i = pl.program_id(0)
    used = i < nu_ref[0]
    new_expert = jnp.logical_or(i == 0, be_ref[i] != be_ref[jnp.maximum(i - 1, 0)])

    @pl.when(jnp.logical_and(used, new_expert))
    def _():
        w1b_ref[...] = w1_ref[0, 0].astype(BF16)
        w2b_ref[...] = w2_ref[0, 0].astype(BF16)

    @pl.when(used)
    def _():
        x = jnp.concatenate([p.astype(BF16) for p in _load_rows(xs_ref, eb, nck)], axis=1)
        gu = _dot(x, w1b_ref[...]) + b1_ref[0, 0]
        g_h = jnp.minimum(gu[:, :f], SWIGLU_LIMIT)
        u_h = jnp.clip(gu[:, f:], -SWIGLU_LIMIT, SWIGLU_LIMIT)
        act = (u_h + 1.0) * (g_h * jax.nn.sigmoid(SWIGLU_ALPHA * g_h))
        _store_rows(ys_ref, _dot(act.astype(BF16), w2b_ref[...]) + b2_ref[0, 0], eb, nck)

    @pl.when(jnp.logical_not(used))
    def _():
        ys_ref[...] = jnp.zeros_like(ys_ref)


def _experts(xs, block_expert, n_used, layer, w1, b1, w2, b2, eb):
    depth, n_exp, d, f2 = w1.shape
    f = f2 // 2
    nck = d // LANES
    blk = (eb * nck, LANES)
    per_expert = lambda *tail: pl.BlockSpec((1, 1, *tail), lambda i, be, nu: (layer, be[i], 0, 0))
    grid_spec = pltpu.PrefetchScalarGridSpec(
        num_scalar_prefetch=2,
        grid=(xs.shape[0] // blk[0],),
        in_specs=[pl.BlockSpec(blk, lambda i, be, nu: (jnp.minimum(i, nu[0] - 1), 0)),
                  per_expert(d, f2), per_expert(1, f2), per_expert(f, d), per_expert(1, d)],
        out_specs=pl.BlockSpec(blk, lambda i, be, nu: (i, 0)),
        scratch_shapes=[pltpu.VMEM((d, f2), BF16), pltpu.VMEM((f, d), BF16)],
    )
    return pl.pallas_call(
        functools.partial(_expert_kernel, eb=eb, f=f, nck=nck),
        grid_spec=grid_spec,
        out_shape=jax.ShapeDtypeStruct(xs.shape, F32),
        compiler_params=_params("arbitrary"),
        name="moe_experts",
    )(block_expert, n_used, xs, w1, b1.reshape(depth, n_exp, 1, f2), w2, b2.reshape(depth, n_exp, 1, d))


def _combine_kernel(dest_ref, dest_next_ref, x_ref, gate_ref, mod_ref, ys_ref, o_ref, buf, sems,
                    *, tc, nck):
    i = pl.program_id(0)

    def gather(slots_ref, half):
        def issue(t2, c):
            for u in range(ISSUE_UNROLL):
                t = t2 * ISSUE_UNROLL + u
                for k in range(TOP_K):
                    slot = slots_ref[0, 0, t * TOP_K + k]
                    src = ys_ref.at[pl.ds(pl.multiple_of(slot * nck, nck), nck)]
                    dst = buf.at[half, k, pl.ds(pl.multiple_of(t * nck, nck), nck)]
                    pltpu.make_async_copy(src, dst, sems.at[half]).start(priority=k % 2)
            return c

        lax.fori_loop(0, tc // ISSUE_UNROLL, issue, 0)

    @pl.when(i == 0)
    def _():
        gather(dest_ref, 0)

    def step(cur):
        @pl.when(i + 1 < pl.num_programs(0))
        def _():
            gather(dest_next_ref, 1 - cur)

        for k in range(TOP_K):
            pltpu.make_async_copy(ys_ref.at[pl.ds(0, tc * nck)], buf.at[cur, k], sems.at[cur]).wait()
        g = gate_ref[...]
        parts = [_load_rows(buf, tc, nck, lead=(cur, k)) for k in range(TOP_K)]
        for c in range(nck):
            lanes = slice(c * LANES, (c + 1) * LANES)
            moe = ((g[:, 0:1] * parts[0][c] + g[:, 1:2] * parts[1][c])
                   + (g[:, 2:3] * parts[2][c] + g[:, 3:4] * parts[3][c]))
            o_ref[:, lanes] = x_ref[:, lanes] + mod_ref[0, 5:6, lanes] * moe

    for half in range(2):
        pl.when(i % 2 == half)(functools.partial(step, half))


def _combine(x2, gates_tok, dest_tok, mod, ys, seq, tc):
    n, d = x2.shape
    nck = d // LANES
    per_b = seq // tc
    n_tiles = n // tc
    slots = dest_tok.reshape(n_tiles, 1, tc * TOP_K)
    return pl.pallas_call(
        functools.partial(_combine_kernel, tc=tc, nck=nck),
        grid=(n_tiles,),
        in_specs=[pl.BlockSpec((1, 1, tc * TOP_K), lambda i: (i, 0, 0), memory_space=pltpu.SMEM),
                  pl.BlockSpec((1, 1, tc * TOP_K), lambda i: (jnp.minimum(i + 1, n_tiles - 1), 0, 0),
                               memory_space=pltpu.SMEM),
                  pl.BlockSpec((tc, d), lambda i: (i, 0)),
                  pl.BlockSpec((tc, TOP_K), lambda i: (i, 0)),
                  pl.BlockSpec((1, 6, d), lambda i: (i // per_b, 0, 0)),
                  pl.BlockSpec(memory_space=pl.ANY)],
        out_specs=pl.BlockSpec((tc, d), lambda i: (i, 0)),
        out_shape=jax.ShapeDtypeStruct((n, d), F32),
        scratch_shapes=[pltpu.VMEM((2, TOP_K, tc * nck, LANES), F32), pltpu.SemaphoreType.DMA((2,))],
        compiler_params=_params("arbitrary"),
        name="moe_combine",
    )(slots, slots, x2, gates_tok, mod, ys)


def _moe(x, routed, mod, layer, w1, b1, w2, b2, tiles):
    b, s, d = x.shape
    n = b * s
    h, idx, gates, rank, cnt = routed
    n_exp = cnt.shape[0]
    eb = tiles["eb"]
    x2 = x.reshape(n, d)
    counts = cnt[:, 0]
    padded = (counts + eb - 1) // eb * eb
    pad_end = jnp.cumsum(padded)
    pad_start = pad_end - padded
    n_blocks = n * TOP_K // eb + n_exp
    is_e = idx[None] == jnp.arange(n_exp, dtype=jnp.int32)[:, None, None]
    dest = rank + jnp.sum(jnp.where(is_e, pad_start[:, None, None], 0), axis=0)
    dest_tok = dest.T.reshape(-1)
    block_row0 = jnp.arange(n_blocks, dtype=jnp.int32) * eb
    block_expert = jnp.minimum(jnp.sum(pad_end[None, :] <= block_row0[:, None], axis=1),
                               n_exp - 1).astype(jnp.int32)
    n_used = (pad_end[-1:] // eb).astype(jnp.int32)
    fill_start = pad_start + counts
    xs = _dispatch(h, dest_tok, fill_start, pad_end - fill_start, n_used, n_blocks, eb, tiles["td"])
    ys = _experts(xs, block_expert, n_used, layer, w1, b1, w2, b2, eb)
    out = _combine(x2, gates.T, dest_tok, mod, ys, s, tiles["tc"])
    return out.reshape(b, s, d)


def _tiles(seq):
    tq = min(seq, 256)
    return {"ts": min(seq, 512), "tq": tq, "nq": min(seq // tq, 16),
            "td": min(seq, 1024), "tc": min(seq, 256), "eb": 512}


def kernel(x, c, norm1_g, norm2_g, ada_w, ada_b, conv_w_in, conv_w, conv_w_out, attn_w_qkv, attn_q_g, attn_k_g, attn_w_o, router_w, router_b, moe_w1, moe_b1, moe_w2, moe_b2):
    depth = ada_w.shape[0]
    t = _tiles(x.shape[1])
    mods = _ada(c, ada_w, ada_b)
    for i in range(depth):
        mod = mods[i]
        g1 = norm1_g[i][None, :]
        route = (norm2_g[i][None, :], router_w[i], router_b[i])
        j = i // 2
        if i % 2 == 0:
            x, *routed = _conv_mixer(x, mod, g1, conv_w_in[j].astype(BF16), conv_w[j],
                                     conv_w_out[j].astype(BF16), route, t["ts"])
        else:
            q, k, v = _qkv(x, mod, g1, attn_w_qkv[j].astype(BF16), attn_q_g[j], attn_k_g[j], t["ts"])
            o = _attention(q, k, v, t["tq"], t["nq"])
            x, *routed = _oproj(x, o, mod, attn_w_o[j].astype(BF16), route, t["ts"])
        x = _moe(x, routed, mod, i, moe_w1, moe_b1, moe_w2, moe_b2, t)
    return x
```
